```python
import jax, jax.numpy as jnp
from jax import lax
import numpy as np

D_MODEL = 1024
BATCH = 8
SEQ = 4096
DEPTH = 1
DEC_BATCH = 32
DEC_SEQ = 1
PAST_LEN = 16384
PAGE_SIZE = 128

HEAD_DIM = 64
H_A = 8
H_B = 8
W_A = H_A * HEAD_DIM
W_B = H_B * HEAD_DIM
MOBA_BLOCK = 256
MOBA_TOPK = 3
Q_BLOCK = 128
PEER_HEADS = 8
PEER_NKEYS = 128
PEER_EXPERTS = PEER_NKEYS * PEER_NKEYS
PEER_DKEY = 256
PEER_TOPK = 16
PEER_TOKEN_BLOCK = 128
RMS_EPS = 1e-6
FORGET_BIAS = 3.0
COL_QA = 0
COL_KA = COL_QA + W_A
COL_VA = COL_KA + W_A
COL_QB = COL_VA + W_A
COL_KB = COL_QB + W_B
COL_VB = COL_KB + W_B
COL_F = COL_VB + W_B
COL_GA = COL_F + H_B
COL_GB = COL_GA + D_MODEL
IN_COLS = COL_GB + D_MODEL

kernel_name = 'hybrid_moba_fox_peer_step'


def rmsnorm(x, g):
    xf = x.astype(jnp.float32)
    xf = xf * lax.rsqrt(jnp.mean(xf * xf, axis=-1, keepdims=True) + RMS_EPS)
    return xf.astype(x.dtype) * g


def alibi_slopes():
    return 2.0 ** (-8.0 * jnp.arange(1, H_A + 1, dtype=jnp.float32) / H_A)


def split_heads(z, lo, hi, n_heads):
    return z[..., lo:hi].reshape(z.shape[:-1] + (n_heads, HEAD_DIM))


def project(x, g_mix, w_in, b_forget):
    xn = rmsnorm(x, g_mix)
    z = jnp.einsum('btd,dc->btc', xn, w_in)
    qa = split_heads(z, COL_QA, COL_KA, H_A)
    ka = split_heads(z, COL_KA, COL_VA, H_A)
    va = split_heads(z, COL_VA, COL_QB, H_A)
    qb = split_heads(z, COL_QB, COL_KB, H_B)
    kb = split_heads(z, COL_KB, COL_VB, H_B)
    vb = split_heads(z, COL_VB, COL_F, H_B)
    logf = jax.nn.log_sigmoid((z[..., COL_F:COL_GA] + b_forget).astype(jnp.float32))
    ga = jax.nn.sigmoid(z[..., COL_GA:COL_GB])
    gb = jax.nn.sigmoid(z[..., COL_GB:IN_COLS])
    return qa, ka, va, qb, kb, vb, logf, ga, gb


def moba_blocks(k, v):
    L = k.shape[0]
    n_blk = -(-L // MOBA_BLOCK)
    pad = ((0, n_blk * MOBA_BLOCK - L), (0, 0), (0, 0))
    kblk = jnp.pad(k, pad).reshape(n_blk, MOBA_BLOCK, H_A, HEAD_DIM).transpose(2, 0, 1, 3)
    vblk = jnp.pad(v, pad).reshape(n_blk, MOBA_BLOCK, H_A, HEAD_DIM).transpose(2, 0, 1, 3)
    kmean = jnp.mean(kblk.astype(jnp.float32), axis=2)
    return kblk, vblk, kmean


def moba_attend(q, t_pos, kblk, vblk, kmean, slopes):
    n_q = q.shape[0]
    n_blk = kblk.shape[1]
    topk = min(MOBA_TOPK, n_blk)
    own = t_pos // MOBA_BLOCK
    gate = jnp.einsum('qhd,hnd->qhn', q.astype(jnp.float32), kmean)
    fully_past = jnp.arange(n_blk, dtype=jnp.int32)[None, None, :] < own[:, None, None]
    gate = jnp.where(fully_past, gate, -jnp.inf)
    _, sel = lax.top_k(gate, topk)
    sel_ok = sel < own[:, None, None]
    own_b = jnp.broadcast_to(own[:, None, None], (n_q, H_A, 1)).astype(sel.dtype)
    idx = jnp.concatenate([sel, own_b], axis=-1)
    ok = jnp.concatenate([sel_ok, jnp.ones(own_b.shape, dtype=bool)], axis=-1)
    hidx = jnp.arange(H_A)[None, :, None]
    kg = kblk[hidx, idx]
    vg = vblk[hidx, idx]
    s_pos = idx[..., None] * MOBA_BLOCK + jnp.arange(MOBA_BLOCK, dtype=jnp.int32)
    dist = (t_pos[:, None, None, None] - s_pos).astype(jnp.float32)
    logits = jnp.einsum('qhd,qhjsd->qhjs', q, kg).astype(jnp.float32) * (HEAD_DIM ** -0.5)
    logits = logits - slopes[None, :, None, None] * dist
    logits = jnp.where(ok[..., None] & (dist >= 0), logits, -jnp.inf)
    p = jax.nn.softmax(logits.reshape(n_q, H_A, -1), axis=-1).reshape(logits.shape)
    return jnp.einsum('qhjs,qhjsd->qhd', p.astype(vg.dtype), vg)


def fox_attend(q, t_pos, c_q, k, v, c_k):
    L = k.shape[0]
    logits = jnp.einsum('qhd,shd->hqs', q, k).astype(jnp.float32) * (HEAD_DIM ** -0.5)
    logits = logits + c_q.T[:, :, None] - c_k.T[:, None, :]
    causal = jnp.arange(L, dtype=jnp.int32)[None, :] <= t_pos[:, None]
    logits = jnp.where(causal[None], logits, -jnp.inf)
    p = jax.nn.softmax(logits, axis=-1)
    return jnp.einsum('hqs,shd->qhd', p.astype(v.dtype), v)


def moba_prompt(q, k, v, slopes):
    B, T = q.shape[:2]
    n_qb = T // Q_BLOCK
    qc = q.reshape(B, n_qb, Q_BLOCK, H_A, HEAD_DIM)
    pos = jnp.arange(T, dtype=jnp.int32).reshape(n_qb, Q_BLOCK)

    def per_seq(args):
        q_s, k_s, v_s = args
        kblk, vblk, kmean = moba_blocks(k_s, v_s)
        return lax.map(lambda a: moba_attend(a[0], a[1], kblk, vblk, kmean, slopes), (q_s, pos))

    return lax.map(per_seq, (qc, k, v)).reshape(B, T, W_A)


def fox_prompt(q, k, v, logf):
    B, T = q.shape[:2]
    n_qb = T // Q_BLOCK
    c = jnp.cumsum(logf, axis=1)
    qc = q.reshape(B, n_qb, Q_BLOCK, H_B, HEAD_DIM)
    cq = c.reshape(B, n_qb, Q_BLOCK, H_B)
    pos = jnp.arange(T, dtype=jnp.int32).reshape(n_qb, Q_BLOCK)

    def per_seq(args):
        q_s, cq_s, k_s, v_s, c_s = args
        return lax.map(lambda a: fox_attend(a[0], a[1], a[2], k_s, v_s, c_s), (q_s, pos, cq_s))

    return lax.map(per_seq, (qc, cq, k, v, c)).reshape(B, T, W_B)


def attend_sample(qa, ka, va, qb, kb, vb, logf, cache_moba_k, cache_moba_v, cache_fox_k,
                  cache_fox_v, cache_fox_logf, page_table, layer, slopes):
    DB, DS = qa.shape[:2]
    past = page_table.shape[1] * cache_moba_k.shape[2]
    pos = past + jnp.arange(DS, dtype=jnp.int32)

    def per_seq(args):
        qa_s, ka_s, va_s, qb_s, kb_s, vb_s, lf_s, pt = args

        def rows(cache, new):
            r = cache[layer, pt]
            r = r.reshape((past,) + r.shape[2:]).astype(new.dtype)
            return jnp.concatenate([r, new], axis=0)

        kblk, vblk, kmean = moba_blocks(rows(cache_moba_k, ka_s), rows(cache_moba_v, va_s))
        oa = moba_attend(qa_s, pos, kblk, vblk, kmean, slopes)
        c = jnp.cumsum(rows(cache_fox_logf, lf_s), axis=0)
        ob = fox_attend(qb_s, pos, c[past:], rows(cache_fox_k, kb_s), rows(cache_fox_v, vb_s), c)
        return oa, ob

    oa, ob = lax.map(per_seq, (qa, ka, va, qb, kb, vb, logf, page_table))
    return oa.reshape(DB, DS, W_A), ob.reshape(DB, DS, W_B)


def peer_block(xb, w_q, sub_keys, peer_u, peer_v):
    T = xb.shape[0]
    q = (xb @ w_q).reshape(T, PEER_HEADS, 2, PEER_DKEY // 2)
    s = jnp.einsum('thcd,hcnd->thcn', q, sub_keys).astype(jnp.float32)
    s1, i1 = lax.top_k(s[:, :, 0], PEER_TOPK)
    s2, i2 = lax.top_k(s[:, :, 1], PEER_TOPK)
    cand = (s1[..., :, None] + s2[..., None, :]).reshape(T, PEER_HEADS, PEER_TOPK * PEER_TOPK)
    cidx = (i1[..., :, None] * PEER_NKEYS + i2[..., None, :]).reshape(T, PEER_HEADS, PEER_TOPK * PEER_TOPK)
    top, sel = lax.top_k(cand, PEER_TOPK)
    eidx = jnp.take_along_axis(cidx, sel, axis=-1)
    g = jax.nn.softmax(top, axis=-1)
    u = peer_u[eidx]
    v = peer_v[eidx]
    hid = jax.nn.gelu(jnp.einsum('td,thkd->thk', xb, u))
    return jnp.einsum('thk,thkd->td', (g * hid).astype(v.dtype), v)


def peer_ffn(xn, w_q, sub_keys, peer_u, peer_v):
    T = xn.shape[0]
    blk = min(PEER_TOKEN_BLOCK, T)
    n_blk = -(-T // blk)
    xp = jnp.pad(xn, ((0, n_blk * blk - T), (0, 0))).reshape(n_blk, blk, D_MODEL)
    out = lax.map(lambda xb: peer_block(xb, w_q, sub_keys, peer_u, peer_v), xp)
    return out.reshape(n_blk * blk, D_MODEL)[:T]


def merge_and_ffn(x, oa, ob, ga, gb, w_branch_a, w_branch_b, w_out, g_ffn, w_peer_q,
                  peer_sub_keys, peer_u, peer_v):
    mix = ga * (oa @ w_branch_a) + gb * (ob @ w_branch_b)
    h = x + mix @ w_out
    hn = rmsnorm(h, g_ffn)
    return h + peer_ffn(hn.reshape(-1, D_MODEL), w_peer_q, peer_sub_keys, peer_u, peer_v).reshape(h.shape)


def setup_inputs(seed: int = 0) -> dict:
    key = jax.random.key(seed)
    ks = jax.random.split(key, 24)
    f32 = jnp.float32
    n_pages = PAST_LEN // PAGE_SIZE
    n_used = DEC_BATCH * n_pages
    n_phys = n_used + max(1, n_used // 4)

    def nrm(k, shape, scale=1.0):
        return jax.random.normal(k, shape, f32) * scale

    return {
        'x_prompt': nrm(ks[0], (BATCH, SEQ, D_MODEL)),
        'x_sample': nrm(ks[1], (DEC_BATCH, DEC_SEQ, D_MODEL)),
        'cache_moba_k': nrm(ks[2], (DEPTH, n_phys, PAGE_SIZE, H_A, HEAD_DIM)),
        'cache_moba_v': nrm(ks[3], (DEPTH, n_phys, PAGE_SIZE, H_A, HEAD_DIM)),
        'cache_fox_k': nrm(ks[4], (DEPTH, n_phys, PAGE_SIZE, H_B, HEAD_DIM)),
        'cache_fox_v': nrm(ks[5], (DEPTH, n_phys, PAGE_SIZE, H_B, HEAD_DIM)),
        'cache_fox_logf': jax.nn.log_sigmoid(FORGET_BIAS + nrm(ks[6], (DEPTH, n_phys, PAGE_SIZE, H_B))),
        'page_table': jax.random.permutation(ks[7], n_phys)[:n_used].reshape(DEC_BATCH, n_pages).astype(jnp.int32),
        'g_mix': 1.0 + nrm(ks[8], (DEPTH, D_MODEL), 0.02),
        'w_in': nrm(ks[9], (DEPTH, D_MODEL, IN_COLS), D_MODEL ** -0.5),
        'b_forget': FORGET_BIAS + nrm(ks[10], (DEPTH, H_B), 0.1),
        'w_branch_a': nrm(ks[11], (DEPTH, W_A, D_MODEL), W_A ** -0.5),
        'w_branch_b': nrm(ks[12], (DEPTH, W_B, D_MODEL), W_B ** -0.5),
        'w_out': nrm(ks[13], (DEPTH, D_MODEL, D_MODEL), D_MODEL ** -0.5),
        'g_ffn': 1.0 + nrm(ks[14], (DEPTH, D_MODEL), 0.02),
        'w_peer_q': nrm(ks[15], (DEPTH, D_MODEL, PEER_HEADS * PEER_DKEY), D_MODEL ** -0.5),
        'peer_sub_keys': nrm(ks[16], (DEPTH, PEER_HEADS, 2, PEER_NKEYS, PEER_DKEY // 2), (PEER_DKEY // 2) ** -0.5),
        'peer_u': nrm(ks[17], (DEPTH, PEER_EXPERTS, D_MODEL), D_MODEL ** -0.5),
        'peer_v': nrm(ks[18], (DEPTH, PEER_EXPERTS, D_MODEL), PEER_HEADS ** -0.5),
        'g_final': 1.0 + nrm(ks[19], (D_MODEL,), 0.02),
    }


def reference(x_prompt, x_sample, cache_moba_k, cache_moba_v, cache_fox_k, cache_fox_v,
              cache_fox_logf, page_table, g_mix, w_in, b_forget, w_branch_a, w_branch_b, w_out,
              g_ffn, w_peer_q, peer_sub_keys, peer_u, peer_v, g_final):
    slopes = alibi_slopes()
    xp, xs = x_prompt, x_sample
    p_new, s_new = [], []
    for l in range(DEPTH):
        qa, ka, va, qb, kb, vb, lf, ga, gb = project(xp, g_mix[l], w_in[l], b_forget[l])
        oa = moba_prompt(qa, ka, va, slopes)
        ob = fox_prompt(qb, kb, vb, lf)
        xp = merge_and_ffn(xp, oa, ob, ga, gb, w_branch_a[l], w_branch_b[l], w_out[l], g_ffn[l],
                           w_peer_q[l], peer_sub_keys[l], peer_u[l], peer_v[l])
        p_new.append((ka, va, kb, vb, lf))
        qa, ka, va, qb, kb, vb, lf, ga, gb = project(xs, g_mix[l], w_in[l], b_forget[l])
        oa, ob = attend_sample(qa, ka, va, qb, kb, vb, lf, cache_moba_k, cache_moba_v, cache_fox_k,
                               cache_fox_v, cache_fox_logf, page_table, l, slopes)
        xs = merge_and_ffn(xs, oa, ob, ga, gb, w_branch_a[l], w_branch_b[l], w_out[l], g_ffn[l],
                           w_peer_q[l], peer_sub_keys[l], peer_u[l], peer_v[l])
        s_new.append((ka, va, kb, vb, lf))
    y_prompt = rmsnorm(xp, g_final)
    y_sample = rmsnorm(xs, g_final)
    p_moba_k, p_moba_v, p_fox_k, p_fox_v, p_fox_logf = [jnp.stack(r) for r in zip(*p_new)]
    s_moba_k, s_moba_v, s_fox_k, s_fox_v, s_fox_logf = [jnp.stack(r) for r in zip(*s_new)]
    return (y_prompt, y_sample, p_moba_k, p_moba_v, p_fox_k, p_fox_v, p_fox_logf,
            s_moba_k, s_moba_v, s_fox_k, s_fox_v, s_fox_logf)
```

```python
import functools

import jax
import jax.numpy as jnp
from jax import lax
from jax.experimental import pallas as pl
from jax.experimental.pallas import tpu as pltpu

F32 = jnp.float32
BF16 = jnp.bfloat16

HEAD_DIM = 64
N_HEADS = 8
PAIR = 2 * HEAD_DIM
MOBA_BLOCK = 256
MOBA_TOPK = 3
PEER_HEADS = 8
PEER_NKEYS = 128
PEER_TOPK = 16
RMS_EPS = 1e-6
NEG = -1e30
LANES = 128
SUBLANES = 8
VMEM_LIMIT = 56 * 1024 * 1024

_NT = (((1,), (1,)), ((), ()))


def _dot(a, b):
    return jnp.dot(a, b, preferred_element_type=F32)


def _dot_nt(a, b, precision=None):
    return lax.dot_general(a, b, _NT, precision=precision, preferred_element_type=F32)


def _params(*sem):
    return pltpu.CompilerParams(dimension_semantics=sem, vmem_limit_bytes=VMEM_LIMIT)


def _half(idx):
    return jnp.where(idx >= HEAD_DIM, 1, 0)


def _rms(x, g):
    return x * lax.rsqrt(jnp.mean(x * x, axis=-1, keepdims=True) + RMS_EPS) * g


def _proj_kernel(x_ref, g_ref, wqkv_ref, wfh_ref, wfl_ref, bf_ref, wg_ref,
                 qa_ref, ka_ref, va_ref, qb_ref, kb_ref, vb_ref, lf_ref, ga_ref, gb_ref):
    xn = _rms(x_ref[...], g_ref[...])
    xb = xn.astype(BF16)
    width = qa_ref.shape[-1]
    for i, o_ref in enumerate((qa_ref, ka_ref, va_ref, qb_ref, kb_ref, vb_ref)):
        o_ref[...] = _dot(xb, wqkv_ref[:, i * width:(i + 1) * width])
    xl = (xn - xb.astype(F32)).astype(BF16)
    zf = _dot(xb, wfh_ref[...]) + (_dot(xl, wfh_ref[...]) + _dot(xb, wfl_ref[...]))
    zf = zf[:, :N_HEADS] + bf_ref[...]
    lf_ref[...] = jnp.minimum(zf, 0.0) - jnp.log1p(jnp.exp(-jnp.abs(zf)))
    d = ga_ref.shape[-1]
    ga_ref[...] = 1.0 / (1.0 + jnp.exp(-_dot(xb, wg_ref[:, :d])))
    gb_ref[...] = 1.0 / (1.0 + jnp.exp(-_dot(xb, wg_ref[:, d:])))


def _proj(x, g, wqkv, wfh, wfl, bfg, wg, tm):
    n, d = x.shape
    width = wqkv.shape[1] // 6
    row = lambda i: (i, 0)
    fix = lambda i: (0, 0)
    outs = [jax.ShapeDtypeStruct((n, width), F32)] * 6 + [
        jax.ShapeDtypeStruct((n, N_HEADS), F32),
        jax.ShapeDtypeStruct((n, d), F32), jax.ShapeDtypeStruct((n, d), F32)]
    return pl.pallas_call(
        _proj_kernel,
        grid=(n // tm,),
        in_specs=[pl.BlockSpec((tm, d), row), pl.BlockSpec((1, d), fix),
                  pl.BlockSpec(wqkv.shape, fix), pl.BlockSpec(wfh.shape, fix),
                  pl.BlockSpec(wfl.shape, fix), pl.BlockSpec((1, N_HEADS), fix),
                  pl.BlockSpec(wg.shape, fix)],
        out_specs=[pl.BlockSpec((tm, width), row)] * 6 + [
            pl.BlockSpec((tm, N_HEADS), row), pl.BlockSpec((tm, d), row), pl.BlockSpec((tm, d), row)],
        out_shape=outs,
        compiler_params=_params("parallel"),
        name="proj",
    )(x, g, wqkv, wfh, wfl, bfg, wg)


def _cumsum_kernel(x_ref, o_ref, carry_ref):
    @pl.when(pl.program_id(0) == 0)
    def _():
        carry_ref[...] = jnp.zeros_like(carry_ref)

    blk = x_ref.shape[0]
    r = lax.broadcasted_iota(jnp.int32, (blk, blk), 0)
    c = lax.broadcasted_iota(jnp.int32, (blk, blk), 1)
    tri = jnp.where(c <= r, 1.0, 0.0).astype(F32)
    y = jnp.dot(tri, x_ref[...], precision=lax.Precision.HIGHEST,
                preferred_element_type=F32) + carry_ref[...]
    o_ref[...] = y
    carry_ref[...] = y[blk - 1:blk, :]


def _cumsum_rows(x, blk=256):
    n, c = x.shape
    return pl.pallas_call(
        _cumsum_kernel,
        grid=(n // blk,),
        in_specs=[pl.BlockSpec((blk, c), lambda i: (i, 0))],
        out_specs=pl.BlockSpec((blk, c), lambda i: (i, 0)),
        out_shape=jax.ShapeDtypeStruct((n, c), F32),
        scratch_shapes=[pltpu.VMEM((1, c), F32)],
        compiler_params=_params("arbitrary"),
        name="cumsum",
    )(x)


def _stage_kv(k_ref, v_ref, kb_s, vt_s, kmean_s):
    nblk = k_ref.shape[0] // MOBA_BLOCK

    for c in range(nblk):
        rows = slice(c * MOBA_BLOCK, (c + 1) * MOBA_BLOCK)
        kc = k_ref[rows, :]
        kb_s[rows, :] = kc.astype(BF16)
        if kmean_s is not None:
            kmean_s[c:c + 1, :] = jnp.sum(kc, axis=0, keepdims=True) * (1.0 / MOBA_BLOCK)
        vt_s[:, rows] = v_ref[rows, :].T.astype(BF16)


def _flash_update(state, lg, vt):
    m, l, acc = state
    mn = jnp.maximum(m, jnp.max(lg, axis=0, keepdims=True))
    alpha = jnp.exp(m - mn)
    p = jnp.exp(lg - mn)
    l = alpha * l + jnp.sum(p, axis=0, keepdims=True)
    acc = alpha * acc + _dot(vt, p.astype(BF16))
    return mn, l, acc


def _flash_first(lg, vt):
    m = jnp.max(lg, axis=0, keepdims=True)
    p = jnp.exp(lg - m)
    return m, jnp.sum(p, axis=0, keepdims=True), _dot(vt, p.astype(BF16))


def _moba_kernel(slopes_ref, q_ref, k_ref, v_ref, o_ref, kb_s, vt_s, kmean_s, sel_s):
    hp = pl.program_id(1)
    own = pl.program_id(2)
    tq = q_ref.shape[0]
    nblk = kmean_s.shape[0]

    @pl.when(own == 0)
    def _():
        _stage_kv(k_ref, v_ref, kb_s, vt_s, kmean_s)

    q = q_ref[...]
    lane_head = _half(lax.broadcasted_iota(jnp.int32, (1, PAIR), 1))
    rel = (lax.broadcasted_iota(jnp.int32, (MOBA_BLOCK, tq), 1)
           - lax.broadcasted_iota(jnp.int32, (MOBA_BLOCK, tq), 0)).astype(F32)
    blk = lax.broadcasted_iota(jnp.int32, (nblk, tq), 0)
    row_half = _half(lax.broadcasted_iota(jnp.int32, (PAIR, tq), 0))
    own0 = pl.multiple_of(own * MOBA_BLOCK, MOBA_BLOCK)
    out_t = None
    for e in range(2):
        slope = slopes_ref[hp * 2 + e]
        qm = jnp.where(lane_head == e, q, 0.0)
        gate = _dot_nt(kmean_s[...], qm, precision=lax.Precision.HIGHEST)
        valid = blk < own
        gate = jnp.where(valid, gate, -jnp.inf)
        rank = jnp.zeros((nblk, tq), F32)
        for mth in range(nblk):
            g_m = gate[mth:mth + 1, :]
            beats = jnp.where(g_m > gate, 1.0, jnp.where((g_m == gate) & (blk > mth), 1.0, 0.0))
            rank = rank + beats
        sel = jnp.where(valid & (rank < MOBA_TOPK), 1.0, 0.0)
        for mth in range(nblk):
            sel_s[mth] = jnp.broadcast_to(sel[mth:mth + 1, :], sel_s.shape[1:])

        qs = (qm * HEAD_DIM ** -0.5).astype(BF16)
        lg = _dot_nt(kb_s[pl.ds(own0, MOBA_BLOCK), :], qs) - slope * rel
        lg = jnp.where(rel >= 0, lg, NEG)
        state = _flash_first(lg, vt_s[:, pl.ds(own0, MOBA_BLOCK)])

        def body(j, state):
            r0 = pl.multiple_of(j * MOBA_BLOCK, MOBA_BLOCK)
            d0 = ((own - j) * MOBA_BLOCK).astype(F32)
            lg = _dot_nt(kb_s[pl.ds(r0, MOBA_BLOCK), :], qs) - slope * (rel + d0)
            lg = jnp.where(sel_s[j][0:1, :] > 0, lg, NEG)
            return _flash_update(state, lg, vt_s[:, pl.ds(r0, MOBA_BLOCK)])

        m, l, acc = lax.fori_loop(0, own, body, state)
        o_e = acc / l
        out_t = o_e if e == 0 else jnp.where(row_half == 0, out_t, o_e)
    o_ref[...] = out_t.T


def _moba_prompt(slopes, q, k, v, batch, seq):
    n, width = q.shape
    tq = MOBA_BLOCK
    nq = seq // tq
    npair = width // PAIR
    return pl.pallas_call(
        _moba_kernel,
        grid=(batch, npair, nq),
        in_specs=[pl.BlockSpec(memory_space=pltpu.SMEM),
                  pl.BlockSpec((tq, PAIR), lambda b, h, i: (b * nq + i, h)),
                  pl.BlockSpec((seq, PAIR), lambda b, h, i: (b, h)),
                  pl.BlockSpec((seq, PAIR), lambda b, h, i: (b, h))],
        out_specs=pl.BlockSpec((tq, PAIR), lambda b, h, i: (b * nq + i, h)),
        out_shape=jax.ShapeDtypeStruct((n, width), F32),
        scratch_shapes=[pltpu.VMEM((seq, PAIR), BF16), pltpu.VMEM((PAIR, seq), BF16),
                        pltpu.VMEM((seq // MOBA_BLOCK, PAIR), F32),
                        pltpu.VMEM((seq // MOBA_BLOCK, SUBLANES, tq), F32)],
        compiler_params=_params("parallel", "parallel", "arbitrary"),
        name="moba",
    )(slopes, q, k, v)


def _fox_kernel(q_ref, k_ref, v_ref, cq_ref, ck_ref, o_ref, kb_s, vt_s):
    own = pl.program_id(2)
    tq = q_ref.shape[0]

    @pl.when(own == 0)
    def _():
        _stage_kv(k_ref, v_ref, kb_s, vt_s, None)

    q = q_ref[...]
    lane_head = _half(lax.broadcasted_iota(jnp.int32, (1, PAIR), 1))
    rel = (lax.broadcasted_iota(jnp.int32, (MOBA_BLOCK, tq), 1)
           - lax.broadcasted_iota(jnp.int32, (MOBA_BLOCK, tq), 0))
    row_half = _half(lax.broadcasted_iota(jnp.int32, (PAIR, tq), 0))
    own0 = pl.multiple_of(own * MOBA_BLOCK, MOBA_BLOCK)
    out_t = None
    for e in range(2):
        qs = (jnp.where(lane_head == e, q, 0.0) * HEAD_DIM ** -0.5).astype(BF16)
        cq = cq_ref[e:e + 1, :]
        lg = _dot_nt(kb_s[pl.ds(own0, MOBA_BLOCK), :], qs) + (cq - ck_ref[pl.ds(own0, MOBA_BLOCK), e:e + 1])
        lg = jnp.where(rel >= 0, lg, NEG)
        state = _flash_first(lg, vt_s[:, pl.ds(own0, MOBA_BLOCK)])

        def body(j, state):
            r0 = pl.multiple_of(j * MOBA_BLOCK, MOBA_BLOCK)
            lg = _dot_nt(kb_s[pl.ds(r0, MOBA_BLOCK), :], qs) + (cq - ck_ref[pl.ds(r0, MOBA_BLOCK), e:e + 1])
            return _flash_update(state, lg, vt_s[:, pl.ds(r0, MOBA_BLOCK)])

        m, l, acc = lax.fori_loop(0, own, body, state)
        o_e = acc / l
        out_t = o_e if e == 0 else jnp.where(row_half == 0, out_t, o_e)
    o_ref[...] = out_t.T


def _fox_prompt(q, k, v, cq, ck, batch, seq):
    n, width = q.shape
    tq = MOBA_BLOCK
    nq = seq // tq
    npair = width // PAIR
    return pl.pallas_call(
        _fox_kernel,
        grid=(batch, npair, nq),
        in_specs=[pl.BlockSpec((tq, PAIR), lambda b, h, i: (b * nq + i, h)),
                  pl.BlockSpec((seq, PAIR), lambda b, h, i: (b, h)),
                  pl.BlockSpec((seq, PAIR), lambda b, h, i: (b, h)),
                  pl.BlockSpec((None, None, 2, tq), lambda b, h, i: (b, h, 0, i)),
                  pl.BlockSpec((None, None, seq, 2), lambda b, h, i: (b, h, 0, 0))],
        out_specs=pl.BlockSpec((tq, PAIR), lambda b, h, i: (b * nq + i, h)),
        out_shape=jax.ShapeDtypeStruct((n, width), F32),
        scratch_shapes=[pltpu.VMEM((seq, PAIR), BF16), pltpu.VMEM((PAIR, seq), BF16)],
        compiler_params=_params("parallel", "parallel", "arbitrary"),
        name="fox",
    )(q, k, v, cq, ck)


def _merge_kernel(x_ref, oa_ref, ob_ref, ga_ref, gb_ref, wa_ref, wb_ref, wo_ref, g_ref, wq_ref, sk_ref,
                  h_ref, hn_ref, st_ref):
    a = _dot(oa_ref[...].astype(BF16), wa_ref[...])
    b = _dot(ob_ref[...].astype(BF16), wb_ref[...])
    mix = ga_ref[...] * a + gb_ref[...] * b
    h = x_ref[...] + _dot(mix.astype(BF16), wo_ref[...])
    h_ref[...] = h
    hn = _rms(h, g_ref[...]).astype(BF16)
    hn_ref[...] = hn
    for i in range(st_ref.shape[0]):
        qp = _dot(hn, wq_ref[:, i * PEER_NKEYS:(i + 1) * PEER_NKEYS]).astype(BF16)
        st_ref[i] = _dot_nt(sk_ref[i], qp)


def _merge(x, oa, ob, ga, gb, wa, wb, wo, g, wq, sk, tm):
    n, d = x.shape
    width = oa.shape[1]
    nsk = sk.shape[0]
    row = lambda i: (i, 0)
    fix = lambda i: (0, 0)
    return pl.pallas_call(
        _merge_kernel,
        grid=(n // tm,),
        in_specs=[pl.BlockSpec((tm, d), row), pl.BlockSpec((tm, width), row), pl.BlockSpec((tm, width), row),
                  pl.BlockSpec((tm, d), row), pl.BlockSpec((tm, d), row),
                  pl.BlockSpec(wa.shape, fix), pl.BlockSpec(wb.shape, fix), pl.BlockSpec(wo.shape, fix),
                  pl.BlockSpec((1, d), fix), pl.BlockSpec(wq.shape, fix),
                  pl.BlockSpec(sk.shape, lambda i: (0, 0, 0))],
        out_specs=[pl.BlockSpec((tm, d), row), pl.BlockSpec((tm, d), row),
                   pl.BlockSpec((nsk, PEER_NKEYS, tm), lambda i: (0, 0, i))],
        out_shape=[jax.ShapeDtypeStruct((n, d), F32), jax.ShapeDtypeStruct((n, d), BF16),
                   jax.ShapeDtypeStruct((nsk, PEER_NKEYS, n), F32)],
        compiler_params=_params("parallel"),
        name="merge",
    )(x, oa, ob, ga, gb, wa, wb, wo, g, wq, sk)


def _cand_rows():
    rows = []
    for j in range(PEER_TOPK):
        rows.append(PEER_TOPK // (j + 1))
    return rows


def _extract_sorted(vals, count, dst_ref=None):
    sub = lax.broadcasted_iota(jnp.int32, vals.shape, 0)
    out = []
    for r in range(count):
        mx = jnp.max(vals, axis=0, keepdims=True)
        out.append(mx)
        if dst_ref is not None:
            dst_ref[r:r + 1, :] = mx
        first = jnp.min(jnp.where(vals == mx, sub, vals.shape[0]), axis=0, keepdims=True)
        vals = jnp.where(sub == first, -jnp.inf, vals)
    return out


def _select_kernel(st_ref, rank2_ref, f2_ref, cnt_ref, c1_ref, a_s, b_s, cand_s):
    rows = _cand_rows()

    def head(h, carry):
        s1 = st_ref[2 * h]
        s2 = st_ref[2 * h + 1]
        _extract_sorted(s1, PEER_TOPK, a_s)
        _extract_sorted(s2, PEER_TOPK, b_s)
        a = a_s[...]
        b = b_s[...]
        cand_s[...] = jnp.full(cand_s.shape, -jnp.inf, F32)
        off = 0
        for j, lim in enumerate(rows):
            cand_s[off:off + lim, :] = a[0:lim, :] + b[j:j + 1, :]
            off += lim
        tops = _extract_sorted(cand_s[...], PEER_TOPK)
        theta = tops[PEER_TOPK - 1]
        z = jnp.ones_like(theta)
        for t in tops[1:]:
            z = z + jnp.exp(t - tops[0])
        cnt = jnp.zeros(s1.shape, F32)
        rank2 = jnp.zeros(s2.shape, F32)
        for j in range(PEER_TOPK):
            bj = b[j:j + 1, :]
            cnt = cnt + jnp.where(s1 + bj >= theta, 1.0, 0.0)
            rank2 = rank2 + jnp.where(bj > s2, 1.0, 0.0)
        cnt_ref[h] = cnt
        rank2_ref[h] = rank2
        f2_ref[h] = jnp.exp(s2 - b[0:1, :])
        c1_ref[h] = jnp.exp(s1 - a[0:1, :]) / z
        return carry

    lax.fori_loop(0, PEER_HEADS, head, 0)


def _select(st, tl=LANES):
    nsk, nk, n = st.shape
    spec_in = pl.BlockSpec((nsk, nk, tl), lambda i: (0, 0, i))
    spec_out = pl.BlockSpec((PEER_HEADS, nk, tl), lambda i: (0, 0, i))
    ncand = -(-sum(_cand_rows()) // 8) * 8
    return pl.pallas_call(
        _select_kernel,
        grid=(n // tl,),
        in_specs=[spec_in],
        out_specs=[spec_out] * 4,
        out_shape=[jax.ShapeDtypeStruct((PEER_HEADS, nk, n), F32)] * 4,
        scratch_shapes=[pltpu.VMEM((PEER_TOPK, tl), F32), pltpu.VMEM((PEER_TOPK, tl), F32),
                        pltpu.VMEM((ncand, tl), F32)],
        compiler_params=_params("parallel"),
        name="select",
    )(st)


def _gelu(x):
    return 0.5 * x * (1.0 + jnp.tanh(0.7978845608028654 * (x + 0.044715 * (x * x * x))))


def _peer_kernel(hn_ref, h_ref, rank2_ref, f2_ref, cnt_ref, c1_ref, u_ref, vt_ref, g_ref, y_ref,
                 acc_s, s_s, p_s):
    ec = pl.program_id(1)
    ecn, tb = s_s.shape
    n_i1 = ecn // PEER_NKEYS
    n_tc = tb // LANES

    @pl.when(ec == 0)
    def _():
        acc_s[...] = jnp.zeros_like(acc_s)

    s_s[...] = _dot_nt(u_ref[...], hn_ref[...])

    assert n_i1 % SUBLANES == 0

    def tile(idx, carry):
        grp = idx // n_tc
        c0 = pl.multiple_of((idx % n_tc) * LANES, LANES)
        cols = pl.ds(c0, LANES)
        g0 = ec * n_i1 + grp * SUBLANES
        cnt_g = [cnt_ref[pl.ds(pl.multiple_of(h * PEER_NKEYS + g0, SUBLANES), SUBLANES), cols]
                 for h in range(PEER_HEADS)]
        c1_g = [c1_ref[pl.ds(pl.multiple_of(h * PEER_NKEYS + g0, SUBLANES), SUBLANES), cols]
                for h in range(PEER_HEADS)]
        for il in range(SUBLANES):
            w = jnp.zeros((PEER_NKEYS, LANES), F32)
            for h in range(PEER_HEADS):
                keys = slice(h * PEER_NKEYS, (h + 1) * PEER_NKEYS)
                w = w + jnp.where(rank2_ref[keys, cols] < cnt_g[h][il:il + 1, :],
                                  f2_ref[keys, cols] * c1_g[h][il:il + 1, :], 0.0)
            rows = pl.ds(pl.multiple_of((grp * SUBLANES + il) * PEER_NKEYS, PEER_NKEYS), PEER_NKEYS)
            p_s[rows, cols] = (w * _gelu(s_s[rows, cols])).astype(BF16)
        return carry

    lax.fori_loop(0, (n_i1 // SUBLANES) * n_tc, tile, 0)
    acc_s[...] += _dot(vt_ref[...], p_s[...])

    @pl.when(ec == pl.num_programs(1) - 1)
    def _():
        y_ref[...] = _rms(h_ref[...] + acc_s[...].T, g_ref[...])


def _peer(hn, h, rank2, f2, cnt, c1, u, vt, g, tb, ecn):
    n, d = h.shape
    ne = u.shape[0]
    tok = lambda t, e: (t, 0)
    sel = pl.BlockSpec((PEER_HEADS * PEER_NKEYS, tb), lambda t, e: (0, t))
    rank2, f2, cnt, c1 = (z.reshape(PEER_HEADS * PEER_NKEYS, n) for z in (rank2, f2, cnt, c1))
    return pl.pallas_call(
        _peer_kernel,
        grid=(n // tb, ne // ecn),
        in_specs=[pl.BlockSpec((tb, d), tok), pl.BlockSpec((tb, d), tok), sel, sel, sel, sel,
                  pl.BlockSpec((ecn, d), lambda t, e: (e, 0)), pl.BlockSpec((d, ecn), lambda t, e: (0, e)),
                  pl.BlockSpec((1, d), lambda t, e: (0, 0))],
        out_specs=pl.BlockSpec((tb, d), tok),
        out_shape=jax.ShapeDtypeStruct((n, d), F32),
        scratch_shapes=[pltpu.VMEM((d, tb), F32), pltpu.VMEM((ecn, tb), F32), pltpu.VMEM((ecn, tb), BF16)],
        compiler_params=_params("parallel", "arbitrary"),
        name="peer",
    )(hn, h, rank2, f2, cnt, c1, u, vt, g)


def _page_spec(pages_per_step, i):
    def index(s, p, pt):
        return (0, pt[s, p * pages_per_step + i], 0, 0, 0)
    return index


def _gather_pages_kernel(pt_ref, *refs):
    n_in = len(refs) - 1
    o_ref = refs[-1]
    for i in range(n_in):
        o_ref[i] = refs[i][...]


def _gather_pages(page_table, cache, per_step=8):
    nseq, npages = page_table.shape
    page, heads = cache.shape[2:]
    grid_spec = pltpu.PrefetchScalarGridSpec(
        num_scalar_prefetch=1,
        grid=(nseq, npages // per_step),
        in_specs=[pl.BlockSpec((None, None, page, heads), (lambda i: lambda s, p, pt: (0, pt[s, p * per_step + i], 0, 0))(i))
                  for i in range(per_step)],
        out_specs=pl.BlockSpec((None, per_step, page, heads), lambda s, p, pt: (s, p, 0, 0)),
    )
    return pl.pallas_call(
        _gather_pages_kernel, grid_spec=grid_spec,
        out_shape=jax.ShapeDtypeStruct((nseq, npages, page, heads), F32),
        compiler_params=_params("parallel", "arbitrary"),
        name="gather_logf",
    )(page_table, *([cache] * per_step))


def _diag_mask(heads, lanes):
    sub = lax.broadcasted_iota(jnp.int32, (heads, lanes), 0)
    lan = lax.broadcasted_iota(jnp.int32, (heads, lanes), 1)
    assert heads & (heads - 1) == 0
    return jnp.bitwise_and(lan, heads - 1) == sub, sub, lan


def _fox_sample_kernel(pt_ref, q_ref, kn_ref, vn_ref, cq_ref, ck_ref, *refs):
    n_pg = (len(refs) - 4) // 2
    k_refs = refs[:n_pg]
    v_refs = refs[n_pg:2 * n_pg]
    o_ref, m_s, l_s, acc_s = refs[2 * n_pg:]
    step = pl.program_id(1)
    q = q_ref[...] * HEAD_DIM ** -0.5

    @pl.when(step == 0)
    def _():
        m_s[...] = jnp.sum(q * kn_ref[...], axis=1, keepdims=True)
        l_s[...] = jnp.ones_like(l_s)
        acc_s[...] = vn_ref[...]

    qb = q.astype(BF16)
    page, heads, hd = k_refs[0].shape
    diag, _, _ = _diag_mask(heads, page * heads)
    cq = cq_ref[...][:, 0:1]
    m, l, acc = m_s[...], l_s[...], acc_s[...]
    for i in range(n_pg):
        k2 = k_refs[i][...].reshape(page * heads, hd).astype(BF16)
        lg = _dot_nt(qb, k2) + (cq - ck_ref[i])
        lg = jnp.where(diag, lg, NEG)
        mn = jnp.maximum(m, jnp.max(lg, axis=1, keepdims=True))
        alpha = jnp.exp(m - mn)
        p = jnp.exp(lg - mn)
        l = alpha * l + jnp.sum(p, axis=1, keepdims=True)
        v2 = v_refs[i][...].reshape(page * heads, hd).astype(BF16)
        acc = alpha * acc + _dot(p.astype(BF16), v2)
        m = mn
    m_s[...], l_s[...], acc_s[...] = m, l, acc

    @pl.when(step == pl.num_programs(1) - 1)
    def _():
        o_ref[...] = acc / l


def _fox_sample(page_table, q, kn, vn, cq, ck, cache_k, cache_v, per_step=4):
    nseq, npages = page_table.shape
    page, heads, hd = cache_k.shape[2:]
    tok = pl.BlockSpec((None, heads, hd), lambda s, p, pt: (s, 0, 0))
    pages = [pl.BlockSpec((None, None, page, heads, hd), _page_spec(per_step, i)) for i in range(per_step)]
    grid_spec = pltpu.PrefetchScalarGridSpec(
        num_scalar_prefetch=1,
        grid=(nseq, npages // per_step),
        in_specs=[tok, tok, tok,
                  pl.BlockSpec((None, heads, LANES), lambda s, p, pt: (s, 0, 0)),
                  pl.BlockSpec((None, per_step, 1, page * heads), lambda s, p, pt: (s, p, 0, 0))] + pages + pages,
        out_specs=tok,
        scratch_shapes=[pltpu.VMEM((heads, 1), F32), pltpu.VMEM((heads, 1), F32), pltpu.VMEM((heads, hd), F32)],
    )
    return pl.pallas_call(
        _fox_sample_kernel, grid_spec=grid_spec,
        out_shape=jax.ShapeDtypeStruct((nseq, heads, hd), F32),
        compiler_params=_params("parallel", "arbitrary"),
        name="fox_sample",
    )(page_table, q, kn, vn, cq, ck, *([cache_k] * per_step), *([cache_v] * per_step))


def _moba_gate_kernel(pt_ref, q_ref, *refs):
    n_pg = len(refs) - 2
    k_refs = refs[:n_pg]
    sel_ref, gate_s = refs[n_pg:]
    step = pl.program_id(1)
    pages_per_block = MOBA_BLOCK // k_refs[0].shape[0]
    blocks_per_step = n_pg // pages_per_block
    q = q_ref[...]
    for i in range(blocks_per_step):
        ksum = jnp.sum(k_refs[i * pages_per_block][...], axis=0)
        for j in range(1, pages_per_block):
            ksum = ksum + jnp.sum(k_refs[i * pages_per_block + j][...], axis=0)
        gate = jnp.sum(q * (ksum * (1.0 / MOBA_BLOCK)), axis=1, keepdims=True)
        gate_s[step * blocks_per_step + i] = jnp.broadcast_to(gate, gate_s.shape[1:])

    @pl.when(step == pl.num_programs(1) - 1)
    def _():
        g = gate_s[...]
        blk = lax.broadcasted_iota(jnp.int32, g.shape, 0)
        for r in range(MOBA_TOPK):
            mx = jnp.max(g, axis=0, keepdims=True)
            first = jnp.min(jnp.where(g == mx, blk, g.shape[0]), axis=0, keepdims=True)
            sel_ref[r] = first[0]
            g = jnp.where(blk == first, -jnp.inf, g)


def _moba_gate(page_table, q, cache_k, per_step=8):
    nseq, npages = page_table.shape
    page, heads, hd = cache_k.shape[2:]
    nblk = npages * page // MOBA_BLOCK
    grid_spec = pltpu.PrefetchScalarGridSpec(
        num_scalar_prefetch=1,
        grid=(nseq, npages // per_step),
        in_specs=[pl.BlockSpec((None, heads, hd), lambda s, p, pt: (s, 0, 0))]
        + [pl.BlockSpec((None, None, page, heads, hd), _page_spec(per_step, i)) for i in range(per_step)],
        out_specs=pl.BlockSpec((None, MOBA_TOPK, heads, LANES), lambda s, p, pt: (s, 0, 0, 0)),
        scratch_shapes=[pltpu.VMEM((nblk, heads, LANES), F32)],
    )
    return pl.pallas_call(
        _moba_gate_kernel, grid_spec=grid_spec,
        out_shape=jax.ShapeDtypeStruct((nseq, MOBA_TOPK, heads, LANES), jnp.int32),
        compiler_params=_params("parallel", "arbitrary"),
        name="moba_gate",
    )(page_table, q, *([cache_k] * per_step))


def _moba_sample_kernel(phys_ref, start_ref, q_ref, kn_ref, vn_ref, slope_ref, k_ref, v_ref, o_ref,
                        m_s, l_s, acc_s, *, past):
    s_id, h, j = pl.program_id(0), pl.program_id(1), pl.program_id(2)
    n_j = pl.num_programs(2)
    q = q_ref[...] * HEAD_DIM ** -0.5

    @pl.when(j == 0)
    def _():
        m_s[...] = jnp.sum(q * kn_ref[...], axis=1, keepdims=True)
        l_s[...] = jnp.ones_like(l_s)
        acc_s[...] = vn_ref[...]

    @pl.when((j == 0) & (h == 0))
    def _():
        o_ref[...] = jnp.zeros_like(o_ref)

    page, heads, hd = k_ref.shape
    diag, sub, lan = _diag_mask(heads, page * heads)
    start = start_ref[(s_id * heads + h) * n_j + j]
    dist = (past - start - lax.shift_right_logical(lan, heads.bit_length() - 1)).astype(F32)
    k2 = k_ref[...].reshape(page * heads, hd).astype(BF16)
    lg = _dot_nt(q.astype(BF16), k2) - slope_ref[...][:, 0:1] * dist
    lg = jnp.where(diag & (sub == h), lg, NEG)
    m, l, acc = m_s[...], l_s[...], acc_s[...]
    mn = jnp.maximum(m, jnp.max(lg, axis=1, keepdims=True))
    alpha = jnp.exp(m - mn)
    p = jnp.exp(lg - mn)
    l = alpha * l + jnp.sum(p, axis=1, keepdims=True)
    v2 = v_ref[...].reshape(page * heads, hd).astype(BF16)
    acc = alpha * acc + _dot(p.astype(BF16), v2)
    m_s[...], l_s[...], acc_s[...] = mn, l, acc

    @pl.when(j == n_j - 1)
    def _():
        row = lax.broadcasted_iota(jnp.int32, o_ref.shape, 0)
        o_ref[...] = jnp.where(row == h, acc / l, o_ref[...])


def _moba_sample(phys, start, q, kn, vn, slopes, cache_k, cache_v, past):
    nseq, heads, hd = q.shape
    page = cache_k.shape[2]
    n_j = phys.shape[0] // (nseq * heads)
    tok = pl.BlockSpec((None, heads, hd), lambda s, h, j, ph, st: (s, 0, 0))
    pg = pl.BlockSpec((None, None, page, heads, hd),
                      lambda s, h, j, ph, st: (0, ph[(s * heads + h) * n_j + j], 0, 0, 0))
    grid_spec = pltpu.PrefetchScalarGridSpec(
        num_scalar_prefetch=2,
        grid=(nseq, heads, n_j),
        in_specs=[tok, tok, tok, pl.BlockSpec((heads, LANES), lambda s, h, j, ph, st: (0, 0)), pg, pg],
        out_specs=tok,
        scratch_shapes=[pltpu.VMEM((heads, 1), F32), pltpu.VMEM((heads, 1), F32), pltpu.VMEM((heads, hd), F32)],
    )
    return pl.pallas_call(
        functools.partial(_moba_sample_kernel, past=past), grid_spec=grid_spec,
        out_shape=jax.ShapeDtypeStruct((nseq, heads, hd), F32),
        compiler_params=_params("parallel", "arbitrary", "arbitrary"),
        name="moba_sample",
    )(phys, start, q, kn, vn, slopes, cache_k, cache_v)


def _pad_rows(x, rows):
    return jnp.pad(x, ((0, rows - x.shape[0]),) + ((0, 0),) * (x.ndim - 1))


def _layer_weights(l, g_mix, w_in, b_forget, w_branch_a, w_branch_b, w_out, g_ffn, w_peer_q, peer_sub_keys,
                   peer_u, peer_v):
    d = w_in.shape[1]
    wa_w = w_branch_a.shape[1]
    n_qkv = 6 * wa_w
    w = w_in[l]
    wf = jnp.pad(w[:, n_qkv:n_qkv + N_HEADS], ((0, 0), (0, LANES - N_HEADS)))
    wfh = wf.astype(BF16)
    wfl = (wf - wfh.astype(F32)).astype(BF16)
    sk = peer_sub_keys[l]
    return dict(
        g_mix=g_mix[l].reshape(1, d), wqkv=w[:, :n_qkv].astype(BF16), wfh=wfh, wfl=wfl,
        bf=b_forget[l].reshape(1, N_HEADS), wg=w[:, n_qkv + N_HEADS:].astype(BF16),
        wa=w_branch_a[l].astype(BF16), wb=w_branch_b[l].astype(BF16), wo=w_out[l].astype(BF16),
        g_ffn=g_ffn[l].reshape(1, d), wq=w_peer_q[l].astype(BF16),
        sk=sk.reshape((sk.shape[0] * sk.shape[1],) + sk.shape[2:]).astype(BF16),
        u=peer_u[l].astype(BF16), vt=peer_v[l].T.astype(BF16))


def _ffn(x, oa, ob, ga, gb, wts, g_out, tm, tb, ecn):
    h, hn, st = _merge(x, oa, ob, ga, gb, wts["wa"], wts["wb"], wts["wo"], wts["g_ffn"], wts["wq"], wts["sk"], tm)
    rank2, f2, cnt, c1 = _select(st)
    return _peer(hn, h, rank2, f2, cnt, c1, wts["u"], wts["vt"], g_out, tb, ecn)


def kernel(x_prompt, x_sample, cache_moba_k, cache_moba_v, cache_fox_k, cache_fox_v, cache_fox_logf, page_table,
           g_mix, w_in, b_forget, w_branch_a, w_branch_b, w_out, g_ffn, w_peer_q, peer_sub_keys, peer_u, peer_v,
           g_final):
    batch, seq, d = x_prompt.shape
    nseq, dec_seq, _ = x_sample.shape
    depth = w_in.shape[0]
    assert depth == 1 and dec_seq == 1, "one layer, one new token per sampled sequence"
    npages, page = page_table.shape[1], cache_moba_k.shape[2]
    past = npages * page
    assert past % MOBA_BLOCK == 0 and past // MOBA_BLOCK >= MOBA_TOPK and seq % MOBA_BLOCK == 0
    slopes = 2.0 ** (-8.0 * jnp.arange(1, N_HEADS + 1, dtype=F32) / N_HEADS)
    wts = _layer_weights(0, g_mix, w_in, b_forget, w_branch_a, w_branch_b, w_out, g_ffn, w_peer_q, peer_sub_keys,
                         peer_u, peer_v)
    g_out = g_final.reshape(1, d)
    proj_w = (wts["g_mix"], wts["wqkv"], wts["wfh"], wts["wfl"], wts["bf"], wts["wg"])
    npair = N_HEADS // 2
    n_exp = wts["u"].shape[0]

    n = batch * seq
    xp = x_prompt.reshape(n, d)
    qa, ka, va, qb, kb, vb, lf, ga, gb = _proj(xp, *proj_w, tm=min(256, n))
    c = _cumsum_rows(lf.reshape(batch, seq, N_HEADS).transpose(1, 0, 2).reshape(seq, batch * N_HEADS))
    c = c.reshape(seq, batch, npair, 2)
    oa = _moba_prompt(slopes, qa, ka, va, batch, seq)
    ob = _fox_prompt(qb, kb, vb, c.transpose(1, 2, 3, 0), c.transpose(1, 2, 0, 3), batch, seq)
    tb = min(512, n)
    y_prompt = _ffn(xp, oa, ob, ga, gb, wts, g_out, min(256, n), tb, min(1024, n_exp)).reshape(batch, seq, d)
    kv_shape = (1, batch, seq, N_HEADS, HEAD_DIM)
    p_new = (ka.reshape(kv_shape), va.reshape(kv_shape), kb.reshape(kv_shape), vb.reshape(kv_shape),
             lf.reshape(1, batch, seq, N_HEADS))

    ns = -(-nseq // LANES) * LANES
    xs = _pad_rows(x_sample.reshape(nseq, d), ns)
    qa, ka, va, qb, kb, vb, lf, ga, gb = _proj(xs, *proj_w, tm=ns)
    hsplit = lambda z: z[:nseq].reshape(nseq, N_HEADS, HEAD_DIM)
    sel = _moba_gate(page_table, hsplit(qa), cache_moba_k)[..., 0]
    ppb = MOBA_BLOCK // page
    sel_pages = sel.transpose(0, 2, 1)[..., None] * ppb + jnp.arange(ppb, dtype=jnp.int32)
    sel_pages = sel_pages.reshape(nseq, N_HEADS * MOBA_TOPK * ppb)
    phys = jnp.take_along_axis(page_table, sel_pages, axis=1).reshape(-1)
    start = (sel_pages * page).reshape(-1)
    slopes_b = jnp.broadcast_to(slopes[:, None], (N_HEADS, LANES))
    oa = _moba_sample(phys, start, hsplit(qa), hsplit(ka), hsplit(va), slopes_b, cache_moba_k, cache_moba_v, past)
    lf_past = _gather_pages(page_table, cache_fox_logf).reshape(nseq, past, N_HEADS)
    lf_all = jnp.concatenate([lf_past, lf[:nseq, None, :]], axis=1)
    tot = -(-(past + 1) // MOBA_BLOCK) * MOBA_BLOCK
    lf_all = jnp.pad(lf_all, ((0, 0), (0, tot - past - 1), (0, 0)))
    cs = _cumsum_rows(lf_all.transpose(1, 0, 2).reshape(tot, nseq * N_HEADS)).reshape(tot, nseq, N_HEADS)
    cq = jnp.broadcast_to(cs[past][:, :, None], (nseq, N_HEADS, LANES))
    ck = cs[:past].transpose(1, 0, 2).reshape(nseq, npages, 1, page * N_HEADS)
    ob = _fox_sample(page_table, hsplit(qb), hsplit(kb), hsplit(vb), cq, ck, cache_fox_k, cache_fox_v)
    oa = _pad_rows(oa.reshape(nseq, N_HEADS * HEAD_DIM), ns)
    ob = _pad_rows(ob.reshape(nseq, N_HEADS * HEAD_DIM), ns)
    y_sample = _ffn(xs, oa, ob, ga, gb, wts, g_out, ns, ns, min(1024, n_exp))[:nseq].reshape(nseq, 1, d)
    kv_shape = (1, nseq, 1, N_HEADS, HEAD_DIM)
    s_new = (ka[:nseq].reshape(kv_shape), va[:nseq].reshape(kv_shape), kb[:nseq].reshape(kv_shape),
             vb[:nseq].reshape(kv_shape), lf[:nseq].reshape(1, nseq, 1, N_HEADS))
    return (y_prompt, y_sample) + p_new + s_new
```

```python
import functools

import jax
import jax.numpy as jnp
from jax import lax
from jax.experimental import pallas as pl
from jax.experimental.pallas import tpu as pltpu

F32 = jnp.float32
BF16 = jnp.bfloat16

HEAD_DIM = 64
N_HEADS = 8
PAIR = 2 * HEAD_DIM
MOBA_BLOCK = 256
MOBA_TOPK = 3
PEER_HEADS = 8
PEER_NKEYS = 128
PEER_TOPK = 16
RMS_EPS = 1e-6
NEG = -1e30
LANES = 128
SUBLANES = 8
BF16_ROWS = 2 * SUBLANES
VMEM_LIMIT = 56 * 1024 * 1024

_NT = (((1,), (1,)), ((), ()))


def _dot(a, b):
    return jnp.dot(a, b, preferred_element_type=F32)


def _dot_nt(a, b, precision=None):
    return lax.dot_general(a, b, _NT, precision=precision, preferred_element_type=F32)


def _params(*sem):
    return pltpu.CompilerParams(dimension_semantics=sem, vmem_limit_bytes=VMEM_LIMIT)


def _half(idx):
    return jnp.where(idx >= HEAD_DIM, 1, 0)


def _rms(x, g):
    return x * lax.rsqrt(jnp.mean(x * x, axis=-1, keepdims=True) + RMS_EPS) * g


def _proj_kernel(x_ref, g_ref, wqkv_ref, wfh_ref, wfl_ref, bf_ref, wg_ref,
                 qa_ref, ka_ref, va_ref, qb_ref, kb_ref, vb_ref, lf_ref, ga_ref, gb_ref):
    xn = _rms(x_ref[...], g_ref[...])
    xb = xn.astype(BF16)
    width = qa_ref.shape[-1]
    for i, o_ref in enumerate((qa_ref, ka_ref, va_ref, qb_ref, kb_ref, vb_ref)):
        o_ref[...] = _dot(xb, wqkv_ref[:, i * width:(i + 1) * width])
    xl = (xn - xb.astype(F32)).astype(BF16)
    zf = _dot(xb, wfh_ref[...]) + (_dot(xl, wfh_ref[...]) + _dot(xb, wfl_ref[...]))
    zf = zf[:, :N_HEADS] + bf_ref[...]
    lf_ref[...] = jnp.minimum(zf, 0.0) - jnp.log1p(jnp.exp(-jnp.abs(zf)))
    d = ga_ref.shape[-1]
    ga_ref[...] = 1.0 / (1.0 + jnp.exp(-_dot(xb, wg_ref[:, :d])))
    gb_ref[...] = 1.0 / (1.0 + jnp.exp(-_dot(xb, wg_ref[:, d:])))


def _proj(x, g, wqkv, wfh, wfl, bfg, wg, tm):
    n, d = x.shape
    width = wqkv.shape[1] // 6
    row = lambda i: (i, 0)
    fix = lambda i: (0, 0)
    outs = [jax.ShapeDtypeStruct((n, width), F32)] * 6 + [
        jax.ShapeDtypeStruct((n, N_HEADS), F32),
        jax.ShapeDtypeStruct((n, d), F32), jax.ShapeDtypeStruct((n, d), F32)]
    return pl.pallas_call(
        _proj_kernel,
        grid=(n // tm,),
        in_specs=[pl.BlockSpec((tm, d), row), pl.BlockSpec((1, d), fix),
                  pl.BlockSpec(wqkv.shape, fix), pl.BlockSpec(wfh.shape, fix),
                  pl.BlockSpec(wfl.shape, fix), pl.BlockSpec((1, N_HEADS), fix),
                  pl.BlockSpec(wg.shape, fix)],
        out_specs=[pl.BlockSpec((tm, width), row)] * 6 + [
            pl.BlockSpec((tm, N_HEADS), row), pl.BlockSpec((tm, d), row), pl.BlockSpec((tm, d), row)],
        out_shape=outs,
        compiler_params=_params("parallel"),
        name="proj",
    )(x, g, wqkv, wfh, wfl, bfg, wg)


def _cumsum_kernel(x_ref, o_ref, carry_ref):
    @pl.when(pl.program_id(0) == 0)
    def _():
        carry_ref[...] = jnp.zeros_like(carry_ref)

    blk = x_ref.shape[0]
    r = lax.broadcasted_iota(jnp.int32, (blk, blk), 0)
    c = lax.broadcasted_iota(jnp.int32, (blk, blk), 1)
    tri = jnp.where(c <= r, 1.0, 0.0).astype(F32)
    y = jnp.dot(tri, x_ref[...], precision=lax.Precision.HIGHEST,
                preferred_element_type=F32) + carry_ref[...]
    o_ref[...] = y
    carry_ref[...] = y[blk - 1:blk, :]


def _cumsum_rows(x, blk=256):
    n, c = x.shape
    return pl.pallas_call(
        _cumsum_kernel,
        grid=(n // blk,),
        in_specs=[pl.BlockSpec((blk, c), lambda i: (i, 0))],
        out_specs=pl.BlockSpec((blk, c), lambda i: (i, 0)),
        out_shape=jax.ShapeDtypeStruct((n, c), F32),
        scratch_shapes=[pltpu.VMEM((1, c), F32)],
        compiler_params=_params("arbitrary"),
        name="cumsum",
    )(x)


def _stage_kv(k_ref, v_ref, kb_s, vt_s, kmean_s):
    nblk = k_ref.shape[0] // MOBA_BLOCK
    for c in range(nblk):
        rows = slice(c * MOBA_BLOCK, (c + 1) * MOBA_BLOCK)
        kc = k_ref[rows, :]
        kb_s[rows, :] = kc.astype(BF16)
        if kmean_s is not None:
            kmean_s[c:c + 1, :] = jnp.sum(kc, axis=0, keepdims=True) * (1.0 / MOBA_BLOCK)
        vt_s[:, rows] = v_ref[rows, :].T.astype(BF16)


def _flash_update(state, lg, vt):
    m, l, acc = state
    mn = jnp.maximum(m, jnp.max(lg, axis=0, keepdims=True))
    alpha = jnp.exp(m - mn)
    p = jnp.exp(lg - mn)
    l = alpha * l + jnp.sum(p, axis=0, keepdims=True)
    acc = alpha * acc + _dot(vt, p.astype(BF16))
    return mn, l, acc


def _flash_first(lg, vt):
    m = jnp.max(lg, axis=0, keepdims=True)
    p = jnp.exp(lg - m)
    return m, jnp.sum(p, axis=0, keepdims=True), _dot(vt, p.astype(BF16))


def _head_rows(e):
    return slice(e * HEAD_DIM, (e + 1) * HEAD_DIM)


def _lane_half(hq):
    return slice(hq * LANES, (hq + 1) * LANES)


def _finish_pair(states, o_ref):
    heads = [jnp.concatenate([acc / l for (_, l, acc) in halves], axis=1) for halves in states]
    o_ref[...] = jnp.concatenate(heads, axis=0).T


def _moba_kernel(slopes_ref, q_ref, k_ref, v_ref, o_ref, kb_s, vt_s, kmean_s, sel_s):
    hp = pl.program_id(1)
    own = pl.program_id(2)
    tq = q_ref.shape[0]
    nblk = kmean_s.shape[0]

    @pl.when(own == 0)
    def _():
        _stage_kv(k_ref, v_ref, kb_s, vt_s, kmean_s)

    q = q_ref[...]
    lane_head = _half(lax.broadcasted_iota(jnp.int32, (1, PAIR), 1))
    nhq = tq // LANES
    rel = [(lax.broadcasted_iota(jnp.int32, (MOBA_BLOCK, LANES), 1) + hq * LANES
            - lax.broadcasted_iota(jnp.int32, (MOBA_BLOCK, LANES), 0)).astype(F32) for hq in range(nhq)]
    blk = lax.broadcasted_iota(jnp.int32, (nblk, tq), 0)
    own0 = pl.multiple_of(own * MOBA_BLOCK, MOBA_BLOCK)
    slopes = [slopes_ref[hp * 2 + e] for e in range(2)]
    qs, states = [], []
    k_own = kb_s[pl.ds(own0, MOBA_BLOCK), :]
    for e in range(2):
        qm = jnp.where(lane_head == e, q, 0.0)
        gate = _dot_nt(kmean_s[...], qm, precision=lax.Precision.HIGHEST)
        valid = blk < own
        gate = jnp.where(valid, gate, -jnp.inf)
        rank = jnp.zeros((nblk, tq), F32)
        for mth in range(nblk):
            g_m = gate[mth:mth + 1, :]
            beats = jnp.where(g_m > gate, 1.0, jnp.where((g_m == gate) & (blk > mth), 1.0, 0.0))
            rank = rank + beats
        sel = jnp.where(valid & (rank < MOBA_TOPK), 1.0, 0.0)
        for mth in range(nblk):
            sel_s[e, mth] = jnp.broadcast_to(sel[mth:mth + 1, :], sel_s.shape[2:])
        qe = (qm * HEAD_DIM ** -0.5).astype(BF16)
        qs.append([qe[_lane_half(hq), :] for hq in range(nhq)])
        v_own = vt_s[_head_rows(e), pl.ds(own0, MOBA_BLOCK)]
        halves = []
        for hq in range(nhq):
            lg = _dot_nt(k_own, qs[e][hq]) - slopes[e] * rel[hq]
            halves.append(_flash_first(jnp.where(rel[hq] >= 0, lg, NEG), v_own))
        states.append(tuple(halves))

    def body(j, states):
        r0 = pl.multiple_of(j * MOBA_BLOCK, MOBA_BLOCK)
        kt = kb_s[pl.ds(r0, MOBA_BLOCK), :]
        d0 = ((own - j) * MOBA_BLOCK).astype(F32)
        new = []
        for e in range(2):
            vt = vt_s[_head_rows(e), pl.ds(r0, MOBA_BLOCK)]
            picked = sel_s[e, j][0:1, :]
            halves = []
            for hq in range(nhq):
                lg = _dot_nt(kt, qs[e][hq]) - slopes[e] * (rel[hq] + d0)
                lg = jnp.where(picked[:, _lane_half(hq)] > 0, lg, NEG)
                halves.append(_flash_update(states[e][hq], lg, vt))
            new.append(tuple(halves))
        return tuple(new)

    states = lax.fori_loop(0, own, body, tuple(states))
    _finish_pair(states, o_ref)


def _moba_prompt(slopes, q, k, v, batch, seq):
    n, width = q.shape
    tq = MOBA_BLOCK
    nq = seq // tq
    npair = width // PAIR
    return pl.pallas_call(
        _moba_kernel,
        grid=(batch, npair, nq),
        in_specs=[pl.BlockSpec(memory_space=pltpu.SMEM),
                  pl.BlockSpec((tq, PAIR), lambda b, h, i: (b * nq + i, h)),
                  pl.BlockSpec((seq, PAIR), lambda b, h, i: (b, h)),
                  pl.BlockSpec((seq, PAIR), lambda b, h, i: (b, h))],
        out_specs=pl.BlockSpec((tq, PAIR), lambda b, h, i: (b * nq + i, h)),
        out_shape=jax.ShapeDtypeStruct((n, width), F32),
        scratch_shapes=[pltpu.VMEM((seq, PAIR), BF16), pltpu.VMEM((PAIR, seq), BF16),
                        pltpu.VMEM((seq // MOBA_BLOCK, PAIR), F32),
                        pltpu.VMEM((2, seq // MOBA_BLOCK, SUBLANES, tq), F32)],
        compiler_params=_params("parallel", "parallel", "arbitrary"),
        name="moba",
    )(slopes, q, k, v)


def _fox_kernel(q_ref, k_ref, v_ref, cq_ref, ck_ref, o_ref, kb_s, vt_s):
    own = pl.program_id(2)
    tq = q_ref.shape[0]

    @pl.when(own == 0)
    def _():
        _stage_kv(k_ref, v_ref, kb_s, vt_s, None)

    q = q_ref[...]
    lane_head = _half(lax.broadcasted_iota(jnp.int32, (1, PAIR), 1))
    nhq = tq // LANES
    rel = [(lax.broadcasted_iota(jnp.int32, (MOBA_BLOCK, LANES), 1) + hq * LANES
            - lax.broadcasted_iota(jnp.int32, (MOBA_BLOCK, LANES), 0)) for hq in range(nhq)]
    own0 = pl.multiple_of(own * MOBA_BLOCK, MOBA_BLOCK)
    k_own = kb_s[pl.ds(own0, MOBA_BLOCK), :]
    qs, cq, states = [], [], []
    for e in range(2):
        qe = (jnp.where(lane_head == e, q, 0.0) * HEAD_DIM ** -0.5).astype(BF16)
        qs.append([qe[_lane_half(hq), :] for hq in range(nhq)])
        cq.append([cq_ref[e:e + 1, _lane_half(hq)] for hq in range(nhq)])
        ck = ck_ref[pl.ds(own0, MOBA_BLOCK), e:e + 1]
        v_own = vt_s[_head_rows(e), pl.ds(own0, MOBA_BLOCK)]
        halves = []
        for hq in range(nhq):
            lg = _dot_nt(k_own, qs[e][hq]) + (cq[e][hq] - ck)
            halves.append(_flash_first(jnp.where(rel[hq] >= 0, lg, NEG), v_own))
        states.append(tuple(halves))

    def body(j, states):
        r0 = pl.multiple_of(j * MOBA_BLOCK, MOBA_BLOCK)
        kt = kb_s[pl.ds(r0, MOBA_BLOCK), :]
        new = []
        for e in range(2):
            vt = vt_s[_head_rows(e), pl.ds(r0, MOBA_BLOCK)]
            ck = ck_ref[pl.ds(r0, MOBA_BLOCK), e:e + 1]
            new.append(tuple(_flash_update(states[e][hq], _dot_nt(kt, qs[e][hq]) + (cq[e][hq] - ck), vt)
                             for hq in range(nhq)))
        return tuple(new)

    states = lax.fori_loop(0, own, body, tuple(states))
    _finish_pair(states, o_ref)


def _fox_prompt(q, k, v, cq, ck, batch, seq):
    n, width = q.shape
    tq = MOBA_BLOCK
    nq = seq // tq
    npair = width // PAIR
    return pl.pallas_call(
        _fox_kernel,
        grid=(batch, npair, nq),
        in_specs=[pl.BlockSpec((tq, PAIR), lambda b, h, i: (b * nq + i, h)),
                  pl.BlockSpec((seq, PAIR), lambda b, h, i: (b, h)),
                  pl.BlockSpec((seq, PAIR), lambda b, h, i: (b, h)),
                  pl.BlockSpec((None, None, 2, tq), lambda b, h, i: (b, h, 0, i)),
                  pl.BlockSpec((None, None, seq, 2), lambda b, h, i: (b, h, 0, 0))],
        out_specs=pl.BlockSpec((tq, PAIR), lambda b, h, i: (b * nq + i, h)),
        out_shape=jax.ShapeDtypeStruct((n, width), F32),
        scratch_shapes=[pltpu.VMEM((seq, PAIR), BF16), pltpu.VMEM((PAIR, seq), BF16)],
        compiler_params=_params("parallel", "parallel", "arbitrary"),
        name="fox",
    )(q, k, v, cq, ck)


def _merge_kernel(x_ref, oa_ref, ob_ref, ga_ref, gb_ref, wa_ref, wb_ref, wo_ref, g_ref, wq_ref, sk_ref,
                  h_ref, hn_ref, st_ref):
    a = _dot(oa_ref[...].astype(BF16), wa_ref[...])
    b = _dot(ob_ref[...].astype(BF16), wb_ref[...])
    mix = ga_ref[...] * a + gb_ref[...] * b
    h = x_ref[...] + _dot(mix.astype(BF16), wo_ref[...])
    h_ref[...] = h
    hn = _rms(h, g_ref[...]).astype(BF16)
    hn_ref[...] = hn
    for i in range(st_ref.shape[0]):
        qp = _dot(hn, wq_ref[:, i * PEER_NKEYS:(i + 1) * PEER_NKEYS]).astype(BF16)
        st_ref[i] = _dot_nt(sk_ref[i], qp)


def _merge(x, oa, ob, ga, gb, wa, wb, wo, g, wq, sk, tm):
    n, d = x.shape
    width = oa.shape[1]
    nsk = sk.shape[0]
    row = lambda i: (i, 0)
    fix = lambda i: (0, 0)
    return pl.pallas_call(
        _merge_kernel,
        grid=(n // tm,),
        in_specs=[pl.BlockSpec((tm, d), row), pl.BlockSpec((tm, width), row), pl.BlockSpec((tm, width), row),
                  pl.BlockSpec((tm, d), row), pl.BlockSpec((tm, d), row),
                  pl.BlockSpec(wa.shape, fix), pl.BlockSpec(wb.shape, fix), pl.BlockSpec(wo.shape, fix),
                  pl.BlockSpec((1, d), fix), pl.BlockSpec(wq.shape, fix),
                  pl.BlockSpec(sk.shape, lambda i: (0, 0, 0))],
        out_specs=[pl.BlockSpec((tm, d), row), pl.BlockSpec((tm, d), row),
                   pl.BlockSpec((nsk, PEER_NKEYS, tm), lambda i: (0, 0, i))],
        out_shape=[jax.ShapeDtypeStruct((n, d), F32), jax.ShapeDtypeStruct((n, d), BF16),
                   jax.ShapeDtypeStruct((nsk, PEER_NKEYS, n), F32)],
        compiler_params=_params("parallel"),
        name="merge",
    )(x, oa, ob, ga, gb, wa, wb, wo, g, wq, sk)


def _cand_rows():
    rows = []
    for j in range(PEER_TOPK):
        rows.append(PEER_TOPK // (j + 1))
    return rows


def _extract_sorted(vals, count, dst_ref=None):
    sub = lax.broadcasted_iota(jnp.int32, vals.shape, 0)
    out = []
    for r in range(count):
        mx = jnp.max(vals, axis=0, keepdims=True)
        out.append(mx)
        if dst_ref is not None:
            dst_ref[r:r + 1, :] = mx
        first = jnp.min(jnp.where(vals == mx, sub, vals.shape[0]), axis=0, keepdims=True)
        vals = jnp.where(sub == first, -jnp.inf, vals)
    return out


def _bf16_twice(x):
    bits = pltpu.bitcast(x.astype(BF16).astype(F32), jnp.uint32)
    return bits | lax.shift_right_logical(bits, jnp.uint32(16))


def _select_head(h, st_ref, rank2_ref, f2_ref, cnt_ref, c1_ref, a_s, b_s, cand_s):
    rows = _cand_rows()
    s1 = st_ref[2 * h]
    s2 = st_ref[2 * h + 1]
    _extract_sorted(s1, PEER_TOPK, a_s)
    _extract_sorted(s2, PEER_TOPK, b_s)
    a = a_s[...]
    b = b_s[...]
    cand_s[...] = jnp.full(cand_s.shape, -jnp.inf, F32)
    off = 0
    for j, lim in enumerate(rows):
        cand_s[off:off + lim, :] = a[0:lim, :] + b[j:j + 1, :]
        off += lim
    tops = _extract_sorted(cand_s[...], PEER_TOPK)
    theta = tops[PEER_TOPK - 1]
    z = jnp.ones_like(theta)
    for t in tops[1:]:
        z = z + jnp.exp(t - tops[0])
    cnt = jnp.zeros(s1.shape, F32)
    rank2 = jnp.zeros(s2.shape, F32)
    for j in range(PEER_TOPK):
        bj = b[j:j + 1, :]
        cnt = cnt + jnp.where(s1 + bj >= theta, 1.0, 0.0)
        rank2 = rank2 + jnp.where(bj > s2, 1.0, 0.0)
    cnt_ref[h] = _bf16_twice(cnt)
    rank2_ref[h] = pltpu.bitcast(rank2.astype(BF16), jnp.uint32)
    f2_ref[h] = pltpu.bitcast(jnp.exp(s2 - b[0:1, :]).astype(BF16), jnp.uint32)
    c1_ref[h] = _bf16_twice(jnp.exp(s1 - a[0:1, :]) / z)


def _select_kernel(st_ref, rank2_ref, f2_ref, cnt_ref, c1_ref, *scratch):
    def pair(hp, carry):
        for e in range(2):
            _select_head(2 * hp + e, st_ref, rank2_ref, f2_ref, cnt_ref, c1_ref, *scratch[3 * e:3 * e + 3])
        return carry

    lax.fori_loop(0, PEER_HEADS // 2, pair, 0)


def _select(st, tl=LANES):
    nsk, nk, n = st.shape
    spec_in = pl.BlockSpec((nsk, nk, tl), lambda i: (0, 0, i))
    spec_out = lambda rows: pl.BlockSpec((PEER_HEADS, rows, tl), lambda i: (0, 0, i))
    ncand = -(-sum(_cand_rows()) // SUBLANES) * SUBLANES
    scratch = [pltpu.VMEM((PEER_TOPK, tl), F32), pltpu.VMEM((PEER_TOPK, tl), F32), pltpu.VMEM((ncand, tl), F32)]
    shape = lambda rows: jax.ShapeDtypeStruct((PEER_HEADS, rows, n), jnp.uint32)
    return pl.pallas_call(
        _select_kernel,
        grid=(n // tl,),
        in_specs=[spec_in],
        out_specs=[spec_out(nk // 2), spec_out(nk // 2), spec_out(nk), spec_out(nk)],
        out_shape=[shape(nk // 2), shape(nk // 2), shape(nk), shape(nk)],
        scratch_shapes=scratch * 2,
        compiler_params=_params("parallel"),
        name="select",
    )(st)


def _gelu(x):
    return 0.5 * x * (1.0 + jnp.tanh(0.7978845608028654 * (x + 0.044715 * (x * x * x))))


def _peer_kernel(hn_ref, h_ref, rank2_ref, f2_ref, cnt_ref, c1_ref, u_ref, vt_ref, g_ref, y_ref, acc_s, p_s):
    ec = pl.program_id(1)
    ecn, tb = p_s.shape
    n_i1 = ecn // PEER_NKEYS
    n_tc = tb // LANES
    assert n_i1 == SUBLANES

    @pl.when(ec == 0)
    def _():
        acc_s[...] = jnp.zeros_like(acc_s)

    hn = hn_ref[...]
    g0 = pl.multiple_of(ec * n_i1, SUBLANES)
    half = n_i1 // 2
    for il in range(n_i1):
        rows = slice(il * PEER_NKEYS, (il + 1) * PEER_NKEYS)
        s = _dot_nt(u_ref[rows, :], hn)
        for tc in range(n_tc):
            cols = slice(tc * LANES, (tc + 1) * LANES)
            cnt, c1 = [], []
            for h in range(PEER_HEADS):
                grp = pl.ds(pl.multiple_of(h * PEER_NKEYS + g0, SUBLANES), SUBLANES)
                for ref, dst in ((cnt_ref, cnt), (c1_ref, c1)):
                    row = jnp.broadcast_to(ref[grp, cols][il:il + 1, :], (SUBLANES, LANES))
                    dst.append(pltpu.bitcast(row, BF16))
            for r in range(PEER_NKEYS // BF16_ROWS):
                w = jnp.zeros((BF16_ROWS, LANES), BF16)
                for h in range(PEER_HEADS):
                    words = slice((h * PEER_NKEYS + r * BF16_ROWS) // 2, (h * PEER_NKEYS + (r + 1) * BF16_ROWS) // 2)
                    rank2 = pltpu.bitcast(rank2_ref[words, cols], BF16)
                    f2 = pltpu.bitcast(f2_ref[words, cols], BF16)
                    w = w + jnp.where(rank2 < cnt[h], f2 * c1[h], 0.0)
                sub = slice(r * BF16_ROWS, (r + 1) * BF16_ROWS)
                p_s[il * PEER_NKEYS + r * BF16_ROWS:il * PEER_NKEYS + (r + 1) * BF16_ROWS, cols] = (
                    w * _gelu(s[sub, cols]).astype(BF16))
        if il % half == half - 1:
            k0 = (il // half) * half * PEER_NKEYS
            ks = slice(k0, k0 + half * PEER_NKEYS)
            acc_s[...] += _dot(vt_ref[:, ks], p_s[ks, :])

    @pl.when(ec == pl.num_programs(1) - 1)
    def _():
        y_ref[...] = _rms(h_ref[...] + acc_s[...].T, g_ref[...])


def _peer(hn, h, rank2, f2, cnt, c1, u, vt, g, tb, ecn):
    n, d = h.shape
    ne = u.shape[0]
    tok = lambda t, e: (t, 0)
    tables = [z.reshape(-1, n) for z in (rank2, f2, cnt, c1)]
    sel = [pl.BlockSpec((z.shape[0], tb), lambda t, e: (0, t)) for z in tables]
    rank2, f2, cnt, c1 = tables
    return pl.pallas_call(
        _peer_kernel,
        grid=(n // tb, ne // ecn),
        in_specs=[pl.BlockSpec((tb, d), tok), pl.BlockSpec((tb, d), tok)] + sel + [
                  pl.BlockSpec((ecn, d), lambda t, e: (e, 0)), pl.BlockSpec((d, ecn), lambda t, e: (0, e)),
                  pl.BlockSpec((1, d), lambda t, e: (0, 0))],
        out_specs=pl.BlockSpec((tb, d), tok),
        out_shape=jax.ShapeDtypeStruct((n, d), F32),
        scratch_shapes=[pltpu.VMEM((d, tb), F32), pltpu.VMEM((ecn, tb), BF16)],
        compiler_params=_params("parallel", "arbitrary"),
        name="peer",
    )(hn, h, rank2, f2, cnt, c1, u, vt, g)


def _page_index(per_step, i):
    def index(s, p, pt):
        return (0, pt[s, p * per_step + i], 0, 0, 0)
    return index


def _gather_pages_kernel(pt_ref, *refs):
    o_ref = refs[-1]
    for i in range(len(refs) - 1):
        o_ref[i] = refs[i][...]


def _gather_pages(page_table, cache_t, per_step=8):
    nseq, npages = page_table.shape
    heads, page = cache_t.shape[2:]
    grid_spec = pltpu.PrefetchScalarGridSpec(
        num_scalar_prefetch=1,
        grid=(nseq, npages // per_step),
        in_specs=[pl.BlockSpec((None, None, heads, page),
                               (lambda i: lambda s, p, pt: (0, pt[s, p * per_step + i], 0, 0))(i))
                  for i in range(per_step)],
        out_specs=pl.BlockSpec((None, per_step, heads, page), lambda s, p, pt: (s, p, 0, 0)),
    )
    return pl.pallas_call(
        _gather_pages_kernel, grid_spec=grid_spec,
        out_shape=jax.ShapeDtypeStruct((nseq, npages, heads, page), F32),
        compiler_params=_params("parallel", "arbitrary"),
        name="gather_logf",
    )(page_table, *([cache_t] * per_step))


def _fox_sample_kernel(pt_ref, q_ref, kn_ref, vn_ref, cq_ref, ck_ref, *refs):
    n_pg = (len(refs) - 4) // 2
    k_refs = refs[:n_pg]
    v_refs = refs[n_pg:2 * n_pg]
    o_ref, m_s, l_s, acc_s = refs[2 * n_pg:]
    step = pl.program_id(1)
    heads = q_ref.shape[0]

    @pl.when(step == 0)
    def _():
        m_s[...] = jnp.full(m_s.shape, NEG, F32)
        l_s[...] = jnp.zeros_like(l_s)
        acc_s[...] = jnp.zeros_like(acc_s)

    for h in range(heads):
        q = q_ref[h] * HEAD_DIM ** -0.5
        bias_q = cq_ref[h:h + 1, :]
        lgs = [jnp.sum(q * k_refs[i][h], axis=0, keepdims=True) + (bias_q - ck_ref[i, h:h + 1, :])
               for i in range(n_pg)]
        top = lgs[0]
        for lg in lgs[1:]:
            top = jnp.maximum(top, lg)
        m = m_s[h:h + 1, :]
        mn = jnp.maximum(m, jnp.max(top, axis=1, keepdims=True))
        alpha = jnp.exp(m - mn)
        l = alpha * l_s[h:h + 1, :]
        acc = alpha * acc_s[h]
        for i in range(n_pg):
            p = jnp.exp(lgs[i] - mn)
            l = l + p
            acc = acc + p * v_refs[i][h]
        m_s[h:h + 1, :] = mn
        l_s[h:h + 1, :] = l
        acc_s[h] = acc

    @pl.when(step == pl.num_programs(1) - 1)
    def _():
        for h in range(heads):
            q = q_ref[h] * HEAD_DIM ** -0.5
            lg_new = jnp.sum(q * kn_ref[h], axis=0, keepdims=True)
            m = m_s[h:h + 1, :]
            mf = jnp.maximum(m, lg_new)
            a = jnp.exp(m - mf)[:, 0:1]
            b = jnp.exp(lg_new - mf)[:, 0:1]
            num = a * jnp.sum(acc_s[h], axis=1, keepdims=True) + b * vn_ref[h][:, 0:1]
            den = a * jnp.sum(l_s[h:h + 1, :], axis=1, keepdims=True) + b
            o_ref[h] = num / den


def _fox_sample(page_table, qb, knb, vnb, cq, ck, cache_k, cache_v, per_step=8):
    nseq, npages = page_table.shape
    heads, hd, page = cache_k.shape[2:]
    tok = pl.BlockSpec((None, heads, hd, LANES), lambda s, p, pt: (s, 0, 0, 0))
    pages = [pl.BlockSpec((None, None, heads, hd, page), _page_index(per_step, i)) for i in range(per_step)]
    grid_spec = pltpu.PrefetchScalarGridSpec(
        num_scalar_prefetch=1,
        grid=(nseq, npages // per_step),
        in_specs=[tok, tok, tok,
                  pl.BlockSpec((None, heads, LANES), lambda s, p, pt: (s, 0, 0)),
                  pl.BlockSpec((None, per_step, heads, page), lambda s, p, pt: (s, p, 0, 0))] + pages + pages,
        out_specs=pl.BlockSpec((None, heads, hd, 1), lambda s, p, pt: (s, 0, 0, 0)),
        scratch_shapes=[pltpu.VMEM((heads, LANES), F32), pltpu.VMEM((heads, LANES), F32),
                        pltpu.VMEM((heads, hd, LANES), F32)],
    )
    return pl.pallas_call(
        _fox_sample_kernel, grid_spec=grid_spec,
        out_shape=jax.ShapeDtypeStruct((nseq, heads, hd, 1), F32),
        compiler_params=_params("parallel", "arbitrary"),
        name="fox_sample",
    )(page_table, qb, knb, vnb, cq, ck, *([cache_k] * per_step), *([cache_v] * per_step))


def _moba_gate_kernel(pt_ref, q_ref, *refs):
    n_pg = len(refs) - 2
    k_refs = refs[:n_pg]
    sel_ref, gate_s = refs[n_pg:]
    step = pl.program_id(1)
    heads, _, page = k_refs[0].shape
    pages_per_block = MOBA_BLOCK // page
    blocks_per_step = n_pg // pages_per_block
    for h in range(heads):
        q = q_ref[h]
        for i in range(blocks_per_step):
            t = jnp.sum(q * k_refs[i * pages_per_block][h], axis=0, keepdims=True)
            for j in range(1, pages_per_block):
                t = t + jnp.sum(q * k_refs[i * pages_per_block + j][h], axis=0, keepdims=True)
            gate = jnp.sum(t, axis=1, keepdims=True) * (1.0 / MOBA_BLOCK)
            gate_s[step * blocks_per_step + i, h:h + 1, :] = jnp.broadcast_to(gate, (1, LANES))

    @pl.when(step == pl.num_programs(1) - 1)
    def _():
        g = gate_s[...]
        blk = lax.broadcasted_iota(jnp.int32, g.shape, 0)
        for r in range(MOBA_TOPK):
            mx = jnp.max(g, axis=0, keepdims=True)
            first = jnp.min(jnp.where(g == mx, blk, g.shape[0]), axis=0, keepdims=True)
            sel_ref[r] = first[0]
            g = jnp.where(blk == first, -jnp.inf, g)


def _moba_gate(page_table, qb, cache_k, per_step=8):
    nseq, npages = page_table.shape
    heads, hd, page = cache_k.shape[2:]
    nblk = npages * page // MOBA_BLOCK
    grid_spec = pltpu.PrefetchScalarGridSpec(
        num_scalar_prefetch=1,
        grid=(nseq, npages // per_step),
        in_specs=[pl.BlockSpec((None, heads, hd, LANES), lambda s, p, pt: (s, 0, 0, 0))]
        + [pl.BlockSpec((None, None, heads, hd, page), _page_index(per_step, i)) for i in range(per_step)],
        out_specs=pl.BlockSpec((None, MOBA_TOPK, heads, LANES), lambda s, p, pt: (s, 0, 0, 0)),
        scratch_shapes=[pltpu.VMEM((nblk, heads, LANES), F32)],
    )
    return pl.pallas_call(
        _moba_gate_kernel, grid_spec=grid_spec,
        out_shape=jax.ShapeDtypeStruct((nseq, MOBA_TOPK, heads, LANES), jnp.int32),
        compiler_params=_params("parallel", "arbitrary"),
        name="moba_gate",
    )(page_table, qb, *([cache_k] * per_step))


def _moba_sample_kernel(phys_ref, start_ref, slopes_ref, q_ref, kn_ref, vn_ref, *refs, past):
    n_pg = (len(refs) - 1) // 2
    k_refs = refs[:n_pg]
    v_refs = refs[n_pg:2 * n_pg]
    o_ref = refs[-1]
    s_id, h = pl.program_id(0), pl.program_id(1)
    heads = pl.num_programs(1)
    q = q_ref[...] * HEAD_DIM ** -0.5
    slope = slopes_ref[h]
    lane = lax.broadcasted_iota(jnp.int32, (1, k_refs[0].shape[-1]), 1)
    lg_new = jnp.sum(q * kn_ref[...], axis=0, keepdims=True)
    lgs = []
    top = None
    for i in range(n_pg):
        start = start_ref[(s_id * heads + h) * n_pg + i]
        dist = (past - start - lane).astype(F32)
        lg = jnp.sum(q * k_refs[i][...], axis=0, keepdims=True) - slope * dist
        lgs.append(lg)
        top = lg if top is None else jnp.maximum(top, lg)
    m = jnp.maximum(lg_new, jnp.max(top, axis=1, keepdims=True))
    b = jnp.exp(lg_new - m)
    l = jnp.zeros_like(top)
    acc = jnp.zeros(q.shape, F32)
    for i in range(n_pg):
        p = jnp.exp(lgs[i] - m)
        l = l + p
        acc = acc + p * v_refs[i][...]
    num = jnp.sum(acc, axis=1, keepdims=True) + b[:, 0:1] * vn_ref[...][:, 0:1]
    den = jnp.sum(l, axis=1, keepdims=True) + b[:, 0:1]
    o_ref[...] = num / den


def _moba_sample(phys, start, slopes, qb, knb, vnb, cache_k, cache_v, past):
    nseq, heads, hd, _ = qb.shape
    page = cache_k.shape[-1]
    n_pg = phys.shape[0] // (nseq * heads)
    tok = pl.BlockSpec((None, None, hd, LANES), lambda s, h, ph, st: (s, h, 0, 0))

    def pg(i):
        return pl.BlockSpec((None, None, None, hd, page),
                            lambda s, h, ph, st: (0, ph[(s * heads + h) * n_pg + i], h, 0, 0))

    pages = [pg(i) for i in range(n_pg)]
    grid_spec = pltpu.PrefetchScalarGridSpec(
        num_scalar_prefetch=2,
        grid=(nseq, heads),
        in_specs=[pl.BlockSpec(memory_space=pltpu.SMEM), tok, tok, tok] + pages + pages,
        out_specs=pl.BlockSpec((None, None, hd, 1), lambda s, h, ph, st: (s, h, 0, 0)),
    )
    return pl.pallas_call(
        functools.partial(_moba_sample_kernel, past=past), grid_spec=grid_spec,
        out_shape=jax.ShapeDtypeStruct((nseq, heads, hd, 1), F32),
        compiler_params=_params("parallel", "arbitrary"),
        name="moba_sample",
    )(phys, start, slopes, qb, knb, vnb, *([cache_k] * n_pg), *([cache_v] * n_pg))


def _pad_rows(x, rows):
    return jnp.pad(x, ((0, rows - x.shape[0]),) + ((0, 0),) * (x.ndim - 1))


def _layer_weights(l, g_mix, w_in, b_forget, w_branch_a, w_branch_b, w_out, g_ffn, w_peer_q, peer_sub_keys,
                   peer_u, peer_v):
    d = w_in.shape[1]
    wa_w = w_branch_a.shape[1]
    n_qkv = 6 * wa_w
    w = w_in[l]
    wf = jnp.pad(w[:, n_qkv:n_qkv + N_HEADS], ((0, 0), (0, LANES - N_HEADS)))
    wfh = wf.astype(BF16)
    wfl = (wf - wfh.astype(F32)).astype(BF16)
    sk = peer_sub_keys[l]
    return dict(
        g_mix=g_mix[l].reshape(1, d), wqkv=w[:, :n_qkv].astype(BF16), wfh=wfh, wfl=wfl,
        bf=b_forget[l].reshape(1, N_HEADS), wg=w[:, n_qkv + N_HEADS:].astype(BF16),
        wa=w_branch_a[l].astype(BF16), wb=w_branch_b[l].astype(BF16), wo=w_out[l].astype(BF16),
        g_ffn=g_ffn[l].reshape(1, d), wq=w_peer_q[l].astype(BF16),
        sk=sk.reshape((sk.shape[0] * sk.shape[1],) + sk.shape[2:]).astype(BF16),
        u=peer_u[l].astype(BF16), vt=peer_v[l].T.astype(BF16))


def _ffn(x, oa, ob, ga, gb, wts, g_out, tm, tb, ecn):
    h, hn, st = _merge(x, oa, ob, ga, gb, wts["wa"], wts["wb"], wts["wo"], wts["g_ffn"], wts["wq"], wts["sk"], tm)
    rank2, f2, cnt, c1 = _select(st)
    return _peer(hn, h, rank2, f2, cnt, c1, wts["u"], wts["vt"], g_out, tb, ecn)


def kernel(x_prompt, x_sample, cache_moba_k, cache_moba_v, cache_fox_k, cache_fox_v, cache_fox_logf, page_table,
           g_mix, w_in, b_forget, w_branch_a, w_branch_b, w_out, g_ffn, w_peer_q, peer_sub_keys, peer_u, peer_v,
           g_final):
    batch, seq, d = x_prompt.shape
    nseq, dec_seq, _ = x_sample.shape
    depth = w_in.shape[0]
    assert depth == 1 and dec_seq == 1, "one layer, one new token per sampled sequence"
    npages, page = page_table.shape[1], cache_moba_k.shape[2]
    past = npages * page
    assert past % MOBA_BLOCK == 0 and past // MOBA_BLOCK >= MOBA_TOPK and seq % MOBA_BLOCK == 0
    assert page == LANES
    slopes = 2.0 ** (-8.0 * jnp.arange(1, N_HEADS + 1, dtype=F32) / N_HEADS)
    wts = _layer_weights(0, g_mix, w_in, b_forget, w_branch_a, w_branch_b, w_out, g_ffn, w_peer_q, peer_sub_keys,
                         peer_u, peer_v)
    g_out = g_final.reshape(1, d)
    proj_w = (wts["g_mix"], wts["wqkv"], wts["wfh"], wts["wfl"], wts["bf"], wts["wg"])
    npair = N_HEADS // 2
    ecn = SUBLANES * PEER_NKEYS

    n = batch * seq
    xp = x_prompt.reshape(n, d)
    qa, ka, va, qb, kb, vb, lf, ga, gb = _proj(xp, *proj_w, tm=min(256, n))
    c = _cumsum_rows(lf.reshape(batch, seq, N_HEADS).transpose(1, 0, 2).reshape(seq, batch * N_HEADS))
    c = c.reshape(seq, batch, npair, 2)
    oa = _moba_prompt(slopes, qa, ka, va, batch, seq)
    ob = _fox_prompt(qb, kb, vb, c.transpose(1, 2, 3, 0), c.transpose(1, 2, 0, 3), batch, seq)
    y_prompt = _ffn(xp, oa, ob, ga, gb, wts, g_out, min(256, n), min(512, n), ecn).reshape(batch, seq, d)
    kv_shape = (1, batch, seq, N_HEADS, HEAD_DIM)
    p_new = (ka.reshape(kv_shape), va.reshape(kv_shape), kb.reshape(kv_shape), vb.reshape(kv_shape),
             lf.reshape(1, batch, seq, N_HEADS))

    ns = -(-nseq // LANES) * LANES
    xs = _pad_rows(x_sample.reshape(nseq, d), ns)
    qa, ka, va, qb, kb, vb, lf, ga, gb = _proj(xs, *proj_w, tm=ns)
    lane_bcast = lambda z: jnp.broadcast_to(z[:nseq].reshape(nseq, N_HEADS, HEAD_DIM, 1),
                                            (nseq, N_HEADS, HEAD_DIM, LANES))
    pages_minor = lambda cache: cache.transpose(0, 1, 3, 4, 2)
    mk, mv = pages_minor(cache_moba_k), pages_minor(cache_moba_v)
    sel = _moba_gate(page_table, lane_bcast(qa), mk)[..., 0]
    ppb = MOBA_BLOCK // page
    sel_pages = sel.transpose(0, 2, 1)[..., None] * ppb + jnp.arange(ppb, dtype=jnp.int32)
    sel_pages = sel_pages.reshape(nseq, N_HEADS * MOBA_TOPK * ppb)
    phys = jnp.take_along_axis(page_table, sel_pages, axis=1).reshape(-1)
    start = (sel_pages * page).reshape(-1)
    oa = _moba_sample(phys, start, slopes, lane_bcast(qa), lane_bcast(ka), lane_bcast(va), mk, mv, past)
    lf_past = _gather_pages(page_table, cache_fox_logf.transpose(0, 1, 3, 2))
    lf_past = lf_past.transpose(0, 1, 3, 2).reshape(nseq, past, N_HEADS)
    lf_all = jnp.concatenate([lf_past, lf[:nseq, None, :]], axis=1)
    tot = -(-(past + 1) // MOBA_BLOCK) * MOBA_BLOCK
    lf_all = jnp.pad(lf_all, ((0, 0), (0, tot - past - 1), (0, 0)))
    cs = _cumsum_rows(lf_all.transpose(1, 0, 2).reshape(tot, nseq * N_HEADS)).reshape(tot, nseq, N_HEADS)
    cq = jnp.broadcast_to(cs[past][:, :, None], (nseq, N_HEADS, LANES))
    ck = cs[:past].reshape(npages, page, nseq, N_HEADS).transpose(2, 0, 3, 1)
    ob = _fox_sample(page_table, lane_bcast(qb), lane_bcast(kb), lane_bcast(vb), cq, ck,
                     pages_minor(cache_fox_k), pages_minor(cache_fox_v))
    oa = _pad_rows(oa.reshape(nseq, N_HEADS * HEAD_DIM), ns)
    ob = _pad_rows(ob.reshape(nseq, N_HEADS * HEAD_DIM), ns)
    y_sample = _ffn(xs, oa, ob, ga, gb, wts, g_out, ns, ns, ecn)[:nseq].reshape(nseq, 1, d)
    kv_shape = (1, nseq, 1, N_HEADS, HEAD_DIM)
    s_new = (ka[:nseq].reshape(kv_shape), va[:nseq].reshape(kv_shape), kb[:nseq].reshape(kv_shape),
             vb[:nseq].reshape(kv_shape), lf[:nseq].reshape(1, nseq, 1, N_HEADS))
    return (y_prompt, y_sample) + p_new + s_new
```

```python
import functools

import jax
import jax.numpy as jnp
from jax import lax
from jax.experimental import pallas as pl
from jax.experimental.pallas import tpu as pltpu

F32 = jnp.float32
BF16 = jnp.bfloat16

HEAD_DIM = 64
N_HEADS = 8
PAIR = 2 * HEAD_DIM
MOBA_BLOCK = 256
MOBA_TOPK = 3
PEER_HEADS = 8
PEER_NKEYS = 128
PEER_TOPK = 16
RMS_EPS = 1e-6
NEG = -1e30
LANES = 128
SUBLANES = 8
BF16_ROWS = 2 * SUBLANES
VMEM_LIMIT = 56 * 1024 * 1024

_NT = (((1,), (1,)), ((), ()))


def _dot(a, b):
    return jnp.dot(a, b, preferred_element_type=F32)


def _dot_nt(a, b, precision=None):
    return lax.dot_general(a, b, _NT, precision=precision, preferred_element_type=F32)


def _params(*sem):
    return pltpu.CompilerParams(dimension_semantics=sem, vmem_limit_bytes=VMEM_LIMIT)


def _half(idx):
    return jnp.where(idx >= HEAD_DIM, 1, 0)


def _rms(x, g):
    return x * lax.rsqrt(jnp.mean(x * x, axis=-1, keepdims=True) + RMS_EPS) * g


def _proj_kernel(x_ref, g_ref, wqkv_ref, wfh_ref, wfl_ref, bf_ref, wg_ref,
                 qa_ref, ka_ref, va_ref, qb_ref, kb_ref, vb_ref, lf_ref, ga_ref, gb_ref):
    xn = _rms(x_ref[...], g_ref[...])
    xb = xn.astype(BF16)
    width = qa_ref.shape[-1]
    for i, o_ref in enumerate((qa_ref, ka_ref, va_ref, qb_ref, kb_ref, vb_ref)):
        o_ref[...] = _dot(xb, wqkv_ref[:, i * width:(i + 1) * width])
    xl = (xn - xb.astype(F32)).astype(BF16)
    zf = _dot(xb, wfh_ref[...]) + (_dot(xl, wfh_ref[...]) + _dot(xb, wfl_ref[...]))
    zf = zf[:, :N_HEADS] + bf_ref[...]
    lf_ref[...] = jnp.minimum(zf, 0.0) - jnp.log1p(jnp.exp(-jnp.abs(zf)))
    d = ga_ref.shape[-1]
    ga_ref[...] = 1.0 / (1.0 + jnp.exp(-_dot(xb, wg_ref[:, :d])))
    gb_ref[...] = 1.0 / (1.0 + jnp.exp(-_dot(xb, wg_ref[:, d:])))


def _proj(x, g, wqkv, wfh, wfl, bfg, wg, tm):
    n, d = x.shape
    width = wqkv.shape[1] // 6
    row = lambda i: (i, 0)
    fix = lambda i: (0, 0)
    outs = [jax.ShapeDtypeStruct((n, width), F32)] * 6 + [
        jax.ShapeDtypeStruct((n, N_HEADS), F32),
        jax.ShapeDtypeStruct((n, d), F32), jax.ShapeDtypeStruct((n, d), F32)]
    return pl.pallas_call(
        _proj_kernel,
        grid=(n // tm,),
        in_specs=[pl.BlockSpec((tm, d), row), pl.BlockSpec((1, d), fix),
                  pl.BlockSpec(wqkv.shape, fix), pl.BlockSpec(wfh.shape, fix),
                  pl.BlockSpec(wfl.shape, fix), pl.BlockSpec((1, N_HEADS), fix),
                  pl.BlockSpec(wg.shape, fix)],
        out_specs=[pl.BlockSpec((tm, width), row)] * 6 + [
            pl.BlockSpec((tm, N_HEADS), row), pl.BlockSpec((tm, d), row), pl.BlockSpec((tm, d), row)],
        out_shape=outs,
        compiler_params=_params("parallel"),
        name="proj",
    )(x, g, wqkv, wfh, wfl, bfg, wg)


def _cumsum_kernel(x_ref, o_ref, carry_ref):
    @pl.when(pl.program_id(0) == 0)
    def _():
        carry_ref[...] = jnp.zeros_like(carry_ref)

    blk = x_ref.shape[0]
    r = lax.broadcasted_iota(jnp.int32, (blk, blk), 0)
    c = lax.broadcasted_iota(jnp.int32, (blk, blk), 1)
    tri = jnp.where(c <= r, 1.0, 0.0).astype(F32)
    y = jnp.dot(tri, x_ref[...], precision=lax.Precision.HIGHEST,
                preferred_element_type=F32) + carry_ref[...]
    o_ref[...] = y
    carry_ref[...] = y[blk - 1:blk, :]


def _cumsum_rows(x, blk=256):
    n, c = x.shape
    return pl.pallas_call(
        _cumsum_kernel,
        grid=(n // blk,),
        in_specs=[pl.BlockSpec((blk, c), lambda i: (i, 0))],
        out_specs=pl.BlockSpec((blk, c), lambda i: (i, 0)),
        out_shape=jax.ShapeDtypeStruct((n, c), F32),
        scratch_shapes=[pltpu.VMEM((1, c), F32)],
        compiler_params=_params("arbitrary"),
        name="cumsum",
    )(x)


def _stage_kv(k_ref, v_ref, kb_s, vt_s, kmean_s):
    nblk = k_ref.shape[0] // MOBA_BLOCK
    for c in range(nblk):
        rows = slice(c * MOBA_BLOCK, (c + 1) * MOBA_BLOCK)
        kc = k_ref[rows, :]
        kb_s[rows, :] = kc.astype(BF16)
        if kmean_s is not None:
            kmean_s[c:c + 1, :] = jnp.sum(kc, axis=0, keepdims=True) * (1.0 / MOBA_BLOCK)
        vt_s[:, rows] = v_ref[rows, :].T.astype(BF16)


BIG = 1e30
LOG2E = 1.4426950408889634
PAST, OWN, DEAD = 0, 1, 2


def _block_rows(blk):
    return pl.ds(pl.multiple_of(blk * MOBA_BLOCK, MOBA_BLOCK), MOBA_BLOCK)


def _flash_sweep(own, qs, kb_s, vt_s, bias_s, row_fn, col_fn, qk_s, p_s, acc_s, o_ref):
    nhq = len(qs[0])
    chains = [(e, hq) for e in range(2) for hq in range(nhq)]
    rows = _block_rows
    block_of = lambda t: jnp.where(t == 0, own, jnp.maximum(jnp.minimum(t, own) - 1, 0))

    def issue_qk(blk, slot):
        kt = kb_s[rows(blk), :]
        for c, (e, hq) in enumerate(chains):
            qk_s[slot, c] = _dot_nt(kt, qs[e][hq])

    def issue_pv(blk, slot, alphas):
        for c, (e, hq) in enumerate(chains):
            acc_s[c] = alphas[c] * acc_s[c] + _dot(vt_s[_head_rows(e), rows(blk)], p_s[slot, c])

    acc_s[...] = jnp.zeros_like(acc_s)
    p_s[1] = jnp.zeros(p_s.shape[1:], p_s.dtype)
    issue_qk(own, 0)
    row = lambda v: jnp.full((1, LANES), v, F32)
    init = (tuple(row(NEG) for _ in chains), tuple(row(0.0) for _ in chains), tuple(row(1.0) for _ in chains))

    def trip(t, slot, carry):
        ms, ls, alphas = carry
        issue_pv(block_of(jnp.maximum(t - 1, 0)), 1 - slot, alphas)
        issue_qk(block_of(t + 1), 1 - slot)
        blk = block_of(t)
        kind = jnp.where(t > own, DEAD, jnp.where(t == 0, OWN, PAST))
        off = (own - blk) * MOBA_BLOCK
        cols = [None if col_fn is None else col_fn(e, blk) for e in range(2)]
        new_m, new_l, new_a = [], [], []
        for c, (e, hq) in enumerate(chains):
            lg = (qk_s[slot, c] - bias_s[kind, c]) + row_fn(e, hq, blk, off)
            if cols[e] is not None:
                lg = lg - cols[e]
            mn = jnp.maximum(ms[c], jnp.max(lg, axis=0, keepdims=True))
            alpha = jnp.exp2(ms[c] - mn)
            p = jnp.exp2(lg - mn)
            p_s[slot, c] = p.astype(BF16)
            new_m.append(mn)
            new_l.append(alpha * ls[c] + jnp.sum(p, axis=0, keepdims=True))
            new_a.append(alpha)
        return tuple(new_m), tuple(new_l), tuple(new_a)

    n_pairs = own // 2 + 1
    ms, ls, alphas = lax.fori_loop(0, n_pairs, lambda g, c: trip(2 * g + 1, 1, trip(2 * g, 0, c)), init)
    issue_pv(block_of(2 * n_pairs - 1), 1, alphas)
    heads = [jnp.concatenate([acc_s[e * nhq + hq] / ls[e * nhq + hq] for hq in range(nhq)], axis=1)
             for e in range(2)]
    o_ref[...] = jnp.concatenate(heads, axis=0).T


def _head_rows(e):
    return slice(e * HEAD_DIM, (e + 1) * HEAD_DIM)


def _lane_half(hq):
    return slice(hq * LANES, (hq + 1) * LANES)


def _sweep_scratch(tq):
    nchain = 2 * (tq // LANES)
    return [pltpu.VMEM((3, nchain, MOBA_BLOCK, LANES), F32),
            pltpu.VMEM((2, nchain, MOBA_BLOCK, LANES), F32), pltpu.VMEM((2, nchain, MOBA_BLOCK, LANES), BF16),
            pltpu.VMEM((nchain, HEAD_DIM, LANES), F32)]


def _rel_pos(hq):
    return (lax.broadcasted_iota(jnp.int32, (MOBA_BLOCK, LANES), 1) + hq * LANES
            - lax.broadcasted_iota(jnp.int32, (MOBA_BLOCK, LANES), 0))


def _moba_kernel(slopes_ref, q_ref, k_ref, v_ref, o_ref, kb_s, vt_s, kmean_s, sel_s, bias_s, qk_s, p_s, acc_s):
    hp = pl.program_id(1)
    own = pl.program_id(2)
    tq = q_ref.shape[0]
    nblk = kmean_s.shape[0]
    nhq = tq // LANES
    slopes2 = [slopes_ref[hp * 2 + e] * LOG2E for e in range(2)]

    @pl.when(own == 0)
    def _():
        _stage_kv(k_ref, v_ref, kb_s, vt_s, kmean_s)
        for e in range(2):
            for hq in range(nhq):
                rel = _rel_pos(hq)
                alibi = slopes2[e] * rel.astype(F32)
                bias_s[PAST, e * nhq + hq] = alibi
                bias_s[OWN, e * nhq + hq] = jnp.where(rel >= 0, alibi, BIG)
                bias_s[DEAD, e * nhq + hq] = jnp.full(alibi.shape, BIG, F32)

    q = q_ref[...]
    lane_head = _half(lax.broadcasted_iota(jnp.int32, (1, PAIR), 1))
    blk = lax.broadcasted_iota(jnp.int32, (nblk, tq), 0)
    qs = []
    for e in range(2):
        qm = jnp.where(lane_head == e, q, 0.0)
        gate = _dot_nt(kmean_s[...], qm, precision=lax.Precision.HIGHEST)
        valid = blk < own
        gate = jnp.where(valid, gate, -jnp.inf)
        rank = jnp.zeros((nblk, tq), F32)
        for mth in range(nblk):
            g_m = gate[mth:mth + 1, :]
            beats = jnp.where(g_m > gate, 1.0, jnp.where((g_m == gate) & (blk > mth), 1.0, 0.0))
            rank = rank + beats
        sel = jnp.where((valid & (rank < MOBA_TOPK)) | (blk == own), 1.0, 0.0)
        for mth in range(nblk):
            sel_s[e, mth] = jnp.broadcast_to(sel[mth:mth + 1, :], sel_s.shape[2:])
        qe = (qm * (HEAD_DIM ** -0.5 * LOG2E)).astype(BF16)
        qs.append([qe[_lane_half(hq), :] for hq in range(nhq)])

    def row_fn(e, hq, b, off):
        picked = sel_s[e, b][0:1, _lane_half(hq)]
        return (picked - 1.0) * BIG - slopes2[e] * off.astype(F32)

    _flash_sweep(own, qs, kb_s, vt_s, bias_s, row_fn, None, qk_s, p_s, acc_s, o_ref)


def _moba_prompt(slopes, q, k, v, batch, seq):
    n, width = q.shape
    tq = MOBA_BLOCK
    nq = seq // tq
    npair = width // PAIR
    return pl.pallas_call(
        _moba_kernel,
        grid=(batch, npair, nq),
        in_specs=[pl.BlockSpec(memory_space=pltpu.SMEM),
                  pl.BlockSpec((tq, PAIR), lambda b, h, i: (b * nq + i, h)),
                  pl.BlockSpec((seq, PAIR), lambda b, h, i: (b, h)),
                  pl.BlockSpec((seq, PAIR), lambda b, h, i: (b, h))],
        out_specs=pl.BlockSpec((tq, PAIR), lambda b, h, i: (b * nq + i, h)),
        out_shape=jax.ShapeDtypeStruct((n, width), F32),
        scratch_shapes=[pltpu.VMEM((seq, PAIR), BF16), pltpu.VMEM((PAIR, seq), BF16),
                        pltpu.VMEM((seq // MOBA_BLOCK, PAIR), F32),
                        pltpu.VMEM((2, seq // MOBA_BLOCK, SUBLANES, tq), F32)] + _sweep_scratch(tq),
        compiler_params=_params("parallel", "parallel", "arbitrary"),
        name="moba",
    )(slopes, q, k, v)


def _fox_kernel(q_ref, k_ref, v_ref, cq_ref, ck_ref, o_ref, kb_s, vt_s, ck2_s, bias_s, qk_s, p_s, acc_s):
    own = pl.program_id(2)
    tq = q_ref.shape[0]
    nhq = tq // LANES

    @pl.when(own == 0)
    def _():
        _stage_kv(k_ref, v_ref, kb_s, vt_s, None)
        ck2_s[...] = ck_ref[...] * LOG2E
        for c in range(2 * nhq):
            bias_s[PAST, c] = jnp.zeros(bias_s.shape[2:], F32)
            bias_s[OWN, c] = jnp.where(_rel_pos(c % nhq) >= 0, 0.0, BIG)
            bias_s[DEAD, c] = jnp.full(bias_s.shape[2:], BIG, F32)

    q = q_ref[...]
    lane_head = _half(lax.broadcasted_iota(jnp.int32, (1, PAIR), 1))
    qs, cq2 = [], []
    for e in range(2):
        qe = (jnp.where(lane_head == e, q, 0.0) * (HEAD_DIM ** -0.5 * LOG2E)).astype(BF16)
        qs.append([qe[_lane_half(hq), :] for hq in range(nhq)])
        cq2.append([cq_ref[e:e + 1, _lane_half(hq)] * LOG2E for hq in range(nhq)])

    _flash_sweep(own, qs, kb_s, vt_s, bias_s, lambda e, hq, b, off: cq2[e][hq],
                 lambda e, b: ck2_s[_block_rows(b), e:e + 1], qk_s, p_s, acc_s, o_ref)


def _fox_prompt(q, k, v, cq, ck, batch, seq):
    n, width = q.shape
    tq = MOBA_BLOCK
    nq = seq // tq
    npair = width // PAIR
    return pl.pallas_call(
        _fox_kernel,
        grid=(batch, npair, nq),
        in_specs=[pl.BlockSpec((tq, PAIR), lambda b, h, i: (b * nq + i, h)),
                  pl.BlockSpec((seq, PAIR), lambda b, h, i: (b, h)),
                  pl.BlockSpec((seq, PAIR), lambda b, h, i: (b, h)),
                  pl.BlockSpec((None, None, 2, tq), lambda b, h, i: (b, h, 0, i)),
                  pl.BlockSpec((None, None, seq, 2), lambda b, h, i: (b, h, 0, 0))],
        out_specs=pl.BlockSpec((tq, PAIR), lambda b, h, i: (b * nq + i, h)),
        out_shape=jax.ShapeDtypeStruct((n, width), F32),
        scratch_shapes=[pltpu.VMEM((seq, PAIR), BF16), pltpu.VMEM((PAIR, seq), BF16),
                        pltpu.VMEM((seq, 2), F32)] + _sweep_scratch(tq),
        compiler_params=_params("parallel", "parallel", "arbitrary"),
        name="fox",
    )(q, k, v, cq, ck)


def _merge_kernel(x_ref, oa_ref, ob_ref, ga_ref, gb_ref, wa_ref, wb_ref, wo_ref, g_ref, wq_ref, sk_ref,
                  h_ref, hn_ref, st_ref):
    a = _dot(oa_ref[...].astype(BF16), wa_ref[...])
    b = _dot(ob_ref[...].astype(BF16), wb_ref[...])
    mix = ga_ref[...] * a + gb_ref[...] * b
    h = x_ref[...] + _dot(mix.astype(BF16), wo_ref[...])
    h_ref[...] = h
    hn = _rms(h, g_ref[...]).astype(BF16)
    hn_ref[...] = hn
    for i in range(st_ref.shape[0]):
        qp = _dot(hn, wq_ref[:, i * PEER_NKEYS:(i + 1) * PEER_NKEYS]).astype(BF16)
        st_ref[i] = _dot_nt(sk_ref[i], qp)


def _merge(x, oa, ob, ga, gb, wa, wb, wo, g, wq, sk, tm):
    n, d = x.shape
    width = oa.shape[1]
    nsk = sk.shape[0]
    row = lambda i: (i, 0)
    fix = lambda i: (0, 0)
    return pl.pallas_call(
        _merge_kernel,
        grid=(n // tm,),
        in_specs=[pl.BlockSpec((tm, d), row), pl.BlockSpec((tm, width), row), pl.BlockSpec((tm, width), row),
                  pl.BlockSpec((tm, d), row), pl.BlockSpec((tm, d), row),
                  pl.BlockSpec(wa.shape, fix), pl.BlockSpec(wb.shape, fix), pl.BlockSpec(wo.shape, fix),
                  pl.BlockSpec((1, d), fix), pl.BlockSpec(wq.shape, fix),
                  pl.BlockSpec(sk.shape, lambda i: (0, 0, 0))],
        out_specs=[pl.BlockSpec((tm, d), row), pl.BlockSpec((tm, d), row),
                   pl.BlockSpec((nsk, PEER_NKEYS, tm), lambda i: (0, 0, i))],
        out_shape=[jax.ShapeDtypeStruct((n, d), F32), jax.ShapeDtypeStruct((n, d), BF16),
                   jax.ShapeDtypeStruct((nsk, PEER_NKEYS, n), F32)],
        compiler_params=_params("parallel"),
        name="merge",
    )(x, oa, ob, ga, gb, wa, wb, wo, g, wq, sk)


def _cand_rows():
    rows = []
    for j in range(PEER_TOPK):
        rows.append(PEER_TOPK // (j + 1))
    return rows


def _extract_sorted(vals, count, dst_ref=None):
    sub = lax.broadcasted_iota(jnp.int32, vals.shape, 0)
    out = []
    for r in range(count):
        mx = jnp.max(vals, axis=0, keepdims=True)
        out.append(mx)
        if dst_ref is not None:
            dst_ref[r:r + 1, :] = mx
        first = jnp.min(jnp.where(vals == mx, sub, vals.shape[0]), axis=0, keepdims=True)
        vals = jnp.where(sub == first, -jnp.inf, vals)
    return out


def _bf16_twice(x):
    bits = pltpu.bitcast(x.astype(BF16).astype(F32), jnp.uint32)
    return bits | lax.shift_right_logical(bits, jnp.uint32(16))


def _select_head(h, st_ref, rank2_ref, f2_ref, cnt_ref, c1_ref, a_s, b_s, cand_s):
    rows = _cand_rows()
    s1 = st_ref[2 * h]
    s2 = st_ref[2 * h + 1]
    _extract_sorted(s1, PEER_TOPK, a_s)
    _extract_sorted(s2, PEER_TOPK, b_s)
    a = a_s[...]
    b = b_s[...]
    cand_s[...] = jnp.full(cand_s.shape, -jnp.inf, F32)
    off = 0
    for j, lim in enumerate(rows):
        cand_s[off:off + lim, :] = a[0:lim, :] + b[j:j + 1, :]
        off += lim
    tops = _extract_sorted(cand_s[...], PEER_TOPK)
    theta = tops[PEER_TOPK - 1]
    z = jnp.ones_like(theta)
    for t in tops[1:]:
        z = z + jnp.exp(t - tops[0])
    cnt = jnp.zeros(s1.shape, F32)
    rank2 = jnp.zeros(s2.shape, F32)
    for j in range(PEER_TOPK):
        bj = b[j:j + 1, :]
        cnt = cnt + jnp.where(s1 + bj >= theta, 1.0, 0.0)
        rank2 = rank2 + jnp.where(bj > s2, 1.0, 0.0)
    cnt_ref[h] = _bf16_twice(cnt)
    rank2_ref[h] = pltpu.bitcast(rank2.astype(BF16), jnp.uint32)
    f2_ref[h] = pltpu.bitcast(jnp.exp(s2 - b[0:1, :]).astype(BF16), jnp.uint32)
    c1_ref[h] = _bf16_twice(jnp.exp(s1 - a[0:1, :]) / z)


def _select_kernel(st_ref, rank2_ref, f2_ref, cnt_ref, c1_ref, *scratch):
    def pair(hp, carry):
        for e in range(2):
            _select_head(2 * hp + e, st_ref, rank2_ref, f2_ref, cnt_ref, c1_ref, *scratch[3 * e:3 * e + 3])
        return carry

    lax.fori_loop(0, PEER_HEADS // 2, pair, 0)


def _select(st, tl=LANES):
    nsk, nk, n = st.shape
    spec_in = pl.BlockSpec((nsk, nk, tl), lambda i: (0, 0, i))
    spec_out = lambda rows: pl.BlockSpec((PEER_HEADS, rows, tl), lambda i: (0, 0, i))
    ncand = -(-sum(_cand_rows()) // SUBLANES) * SUBLANES
    scratch = [pltpu.VMEM((PEER_TOPK, tl), F32), pltpu.VMEM((PEER_TOPK, tl), F32), pltpu.VMEM((ncand, tl), F32)]
    shape = lambda rows: jax.ShapeDtypeStruct((PEER_HEADS, rows, n), jnp.uint32)
    return pl.pallas_call(
        _select_kernel,
        grid=(n // tl,),
        in_specs=[spec_in],
        out_specs=[spec_out(nk // 2), spec_out(nk // 2), spec_out(nk), spec_out(nk)],
        out_shape=[shape(nk // 2), shape(nk // 2), shape(nk), shape(nk)],
        scratch_shapes=scratch * 2,
        compiler_params=_params("parallel"),
        name="select",
    )(st)


def _gelu(x):
    return 0.5 * x * (1.0 + jnp.tanh(0.7978845608028654 * (x + 0.044715 * (x * x * x))))


def _peer_kernel(hn_ref, h_ref, rank2_ref, f2_ref, cnt_ref, c1_ref, u_ref, vt_ref, g_ref, y_ref, acc_s, p_s):
    ec = pl.program_id(1)
    ecn, tb = p_s.shape
    n_i1 = ecn // PEER_NKEYS
    n_tc = tb // LANES
    assert n_i1 == SUBLANES

    @pl.when(ec == 0)
    def _():
        acc_s[...] = jnp.zeros_like(acc_s)

    hn = hn_ref[...]
    g0 = pl.multiple_of(ec * n_i1, SUBLANES)
    half = n_i1 // 2
    for il in range(n_i1):
        rows = slice(il * PEER_NKEYS, (il + 1) * PEER_NKEYS)
        s = _dot_nt(u_ref[rows, :], hn)
        for tc in range(n_tc):
            cols = slice(tc * LANES, (tc + 1) * LANES)
            cnt, c1 = [], []
            for h in range(PEER_HEADS):
                grp = pl.ds(pl.multiple_of(h * PEER_NKEYS + g0, SUBLANES), SUBLANES)
                for ref, dst in ((cnt_ref, cnt), (c1_ref, c1)):
                    row = jnp.broadcast_to(ref[grp, cols][il:il + 1, :], (SUBLANES, LANES))
                    dst.append(pltpu.bitcast(row, BF16))
            for r in range(PEER_NKEYS // BF16_ROWS):
                w = jnp.zeros((BF16_ROWS, LANES), BF16)
                for h in range(PEER_HEADS):
                    words = slice((h * PEER_NKEYS + r * BF16_ROWS) // 2, (h * PEER_NKEYS + (r + 1) * BF16_ROWS) // 2)
                    rank2 = pltpu.bitcast(rank2_ref[words, cols], BF16)
                    f2 = pltpu.bitcast(f2_ref[words, cols], BF16)
                    w = w + jnp.where(rank2 < cnt[h], f2 * c1[h], 0.0)
                sub = slice(r * BF16_ROWS, (r + 1) * BF16_ROWS)
                p_s[il * PEER_NKEYS + r * BF16_ROWS:il * PEER_NKEYS + (r + 1) * BF16_ROWS, cols] = (
                    w * _gelu(s[sub, cols]).astype(BF16))
        if il % half == half - 1:
            k0 = (il // half) * half * PEER_NKEYS
            ks = slice(k0, k0 + half * PEER_NKEYS)
            acc_s[...] += _dot(vt_ref[:, ks], p_s[ks, :])

    @pl.when(ec == pl.num_programs(1) - 1)
    def _():
        y_ref[...] = _rms(h_ref[...] + acc_s[...].T, g_ref[...])


def _peer(hn, h, rank2, f2, cnt, c1, u, vt, g, tb, ecn):
    n, d = h.shape
    ne = u.shape[0]
    tok = lambda t, e: (t, 0)
    tables = [z.reshape(-1, n) for z in (rank2, f2, cnt, c1)]
    sel = [pl.BlockSpec((z.shape[0], tb), lambda t, e: (0, t)) for z in tables]
    rank2, f2, cnt, c1 = tables
    return pl.pallas_call(
        _peer_kernel,
        grid=(n // tb, ne // ecn),
        in_specs=[pl.BlockSpec((tb, d), tok), pl.BlockSpec((tb, d), tok)] + sel + [
                  pl.BlockSpec((ecn, d), lambda t, e: (e, 0)), pl.BlockSpec((d, ecn), lambda t, e: (0, e)),
                  pl.BlockSpec((1, d), lambda t, e: (0, 0))],
        out_specs=pl.BlockSpec((tb, d), tok),
        out_shape=jax.ShapeDtypeStruct((n, d), F32),
        scratch_shapes=[pltpu.VMEM((d, tb), F32), pltpu.VMEM((ecn, tb), BF16)],
        compiler_params=_params("parallel", "arbitrary"),
        name="peer",
    )(hn, h, rank2, f2, cnt, c1, u, vt, g)


def _page_index(per_step, i):
    def index(s, p, pt):
        return (0, pt[s, p * per_step + i], 0, 0, 0)
    return index


def _gather_pages_kernel(pt_ref, *refs):
    o_ref = refs[-1]
    for i in range(len(refs) - 1):
        o_ref[i] = refs[i][...]


def _gather_pages(page_table, cache_t, per_step=8):
    nseq, npages = page_table.shape
    heads, page = cache_t.shape[2:]
    grid_spec = pltpu.PrefetchScalarGridSpec(
        num_scalar_prefetch=1,
        grid=(nseq, npages // per_step),
        in_specs=[pl.BlockSpec((None, None, heads, page),
                               (lambda i: lambda s, p, pt: (0, pt[s, p * per_step + i], 0, 0))(i))
                  for i in range(per_step)],
        out_specs=pl.BlockSpec((None, per_step, heads, page), lambda s, p, pt: (s, p, 0, 0)),
    )
    return pl.pallas_call(
        _gather_pages_kernel, grid_spec=grid_spec,
        out_shape=jax.ShapeDtypeStruct((nseq, npages, heads, page), F32),
        compiler_params=_params("parallel", "arbitrary"),
        name="gather_logf",
    )(page_table, *([cache_t] * per_step))


def _fox_sample_kernel(pt_ref, q_ref, kn_ref, vn_ref, cq_ref, ck_ref, *refs):
    n_pg = (len(refs) - 4) // 2
    k_refs = refs[:n_pg]
    v_refs = refs[n_pg:2 * n_pg]
    o_ref, m_s, l_s, acc_s = refs[2 * n_pg:]
    step = pl.program_id(1)
    heads = q_ref.shape[0]

    @pl.when(step == 0)
    def _():
        m_s[...] = jnp.full(m_s.shape, NEG, F32)
        l_s[...] = jnp.zeros_like(l_s)
        acc_s[...] = jnp.zeros_like(acc_s)

    for h in range(heads):
        q = q_ref[h] * HEAD_DIM ** -0.5
        bias_q = cq_ref[h:h + 1, :]
        lgs = [jnp.sum(q * k_refs[i][h], axis=0, keepdims=True) + (bias_q - ck_ref[i, h:h + 1, :])
               for i in range(n_pg)]
        top = lgs[0]
        for lg in lgs[1:]:
            top = jnp.maximum(top, lg)
        m = m_s[h:h + 1, :]
        mn = jnp.maximum(m, jnp.max(top, axis=1, keepdims=True))
        alpha = jnp.exp(m - mn)
        l = alpha * l_s[h:h + 1, :]
        acc = alpha * acc_s[h]
        for i in range(n_pg):
            p = jnp.exp(lgs[i] - mn)
            l = l + p
            acc = acc + p * v_refs[i][h]
        m_s[h:h + 1, :] = mn
        l_s[h:h + 1, :] = l
        acc_s[h] = acc

    @pl.when(step == pl.num_programs(1) - 1)
    def _():
        for h in range(heads):
            q = q_ref[h] * HEAD_DIM ** -0.5
            lg_new = jnp.sum(q * kn_ref[h], axis=0, keepdims=True)
            m = m_s[h:h + 1, :]
            mf = jnp.maximum(m, lg_new)
            a = jnp.exp(m - mf)[:, 0:1]
            b = jnp.exp(lg_new - mf)[:, 0:1]
            num = a * jnp.sum(acc_s[h], axis=1, keepdims=True) + b * vn_ref[h][:, 0:1]
            den = a * jnp.sum(l_s[h:h + 1, :], axis=1, keepdims=True) + b
            o_ref[h] = num / den


def _fox_sample(page_table, qb, knb, vnb, cq, ck, cache_k, cache_v, per_step=8):
    nseq, npages = page_table.shape
    heads, hd, page = cache_k.shape[2:]
    tok = pl.BlockSpec((None, heads, hd, LANES), lambda s, p, pt: (s, 0, 0, 0))
    pages = [pl.BlockSpec((None, None, heads, hd, page), _page_index(per_step, i)) for i in range(per_step)]
    grid_spec = pltpu.PrefetchScalarGridSpec(
        num_scalar_prefetch=1,
        grid=(nseq, npages // per_step),
        in_specs=[tok, tok, tok,
                  pl.BlockSpec((None, heads, LANES), lambda s, p, pt: (s, 0, 0)),
                  pl.BlockSpec((None, per_step, heads, page), lambda s, p, pt: (s, p, 0, 0))] + pages + pages,
        out_specs=pl.BlockSpec((None, heads, hd, 1), lambda s, p, pt: (s, 0, 0, 0)),
        scratch_shapes=[pltpu.VMEM((heads, LANES), F32), pltpu.VMEM((heads, LANES), F32),
                        pltpu.VMEM((heads, hd, LANES), F32)],
    )
    return pl.pallas_call(
        _fox_sample_kernel, grid_spec=grid_spec,
        out_shape=jax.ShapeDtypeStruct((nseq, heads, hd, 1), F32),
        compiler_params=_params("parallel", "arbitrary"),
        name="fox_sample",
    )(page_table, qb, knb, vnb, cq, ck, *([cache_k] * per_step), *([cache_v] * per_step))


def _moba_gate_kernel(pt_ref, q_ref, *refs):
    n_pg = len(refs) - 2
    k_refs = refs[:n_pg]
    sel_ref, gate_s = refs[n_pg:]
    step = pl.program_id(1)
    heads, _, page = k_refs[0].shape
    pages_per_block = MOBA_BLOCK // page
    blocks_per_step = n_pg // pages_per_block
    for h in range(heads):
        q = q_ref[h]
        for i in range(blocks_per_step):
            t = jnp.sum(q * k_refs[i * pages_per_block][h], axis=0, keepdims=True)
            for j in range(1, pages_per_block):
                t = t + jnp.sum(q * k_refs[i * pages_per_block + j][h], axis=0, keepdims=True)
            gate = jnp.sum(t, axis=1, keepdims=True) * (1.0 / MOBA_BLOCK)
            gate_s[step * blocks_per_step + i, h:h + 1, :] = jnp.broadcast_to(gate, (1, LANES))

    @pl.when(step == pl.num_programs(1) - 1)
    def _():
        g = gate_s[...]
        blk = lax.broadcasted_iota(jnp.int32, g.shape, 0)
        for r in range(MOBA_TOPK):
            mx = jnp.max(g, axis=0, keepdims=True)
            first = jnp.min(jnp.where(g == mx, blk, g.shape[0]), axis=0, keepdims=True)
            sel_ref[r] = first[0]
            g = jnp.where(blk == first, -jnp.inf, g)


def _moba_gate(page_table, qb, cache_k, per_step=8):
    nseq, npages = page_table.shape
    heads, hd, page = cache_k.shape[2:]
    nblk = npages * page // MOBA_BLOCK
    grid_spec = pltpu.PrefetchScalarGridSpec(
        num_scalar_prefetch=1,
        grid=(nseq, npages // per_step),
        in_specs=[pl.BlockSpec((None, heads, hd, LANES), lambda s, p, pt: (s, 0, 0, 0))]
        + [pl.BlockSpec((None, None, heads, hd, page), _page_index(per_step, i)) for i in range(per_step)],
        out_specs=pl.BlockSpec((None, MOBA_TOPK, heads, LANES), lambda s, p, pt: (s, 0, 0, 0)),
        scratch_shapes=[pltpu.VMEM((nblk, heads, LANES), F32)],
    )
    return pl.pallas_call(
        _moba_gate_kernel, grid_spec=grid_spec,
        out_shape=jax.ShapeDtypeStruct((nseq, MOBA_TOPK, heads, LANES), jnp.int32),
        compiler_params=_params("parallel", "arbitrary"),
        name="moba_gate",
    )(page_table, qb, *([cache_k] * per_step))


def _moba_sample_kernel(phys_ref, start_ref, slopes_ref, q_ref, kn_ref, vn_ref, *refs, past):
    n_pg = (len(refs) - 1) // 2
    k_refs = refs[:n_pg]
    v_refs = refs[n_pg:2 * n_pg]
    o_ref = refs[-1]
    s_id, h = pl.program_id(0), pl.program_id(1)
    heads = pl.num_programs(1)
    q = q_ref[...] * HEAD_DIM ** -0.5
    slope = slopes_ref[h]
    lane = lax.broadcasted_iota(jnp.int32, (1, k_refs[0].shape[-1]), 1)
    lg_new = jnp.sum(q * kn_ref[...], axis=0, keepdims=True)
    lgs = []
    top = None
    for i in range(n_pg):
        start = start_ref[(s_id * heads + h) * n_pg + i]
        dist = (past - start - lane).astype(F32)
        lg = jnp.sum(q * k_refs[i][...], axis=0, keepdims=True) - slope * dist
        lgs.append(lg)
        top = lg if top is None else jnp.maximum(top, lg)
    m = jnp.maximum(lg_new, jnp.max(top, axis=1, keepdims=True))
    b = jnp.exp(lg_new - m)
    l = jnp.zeros_like(top)
    acc = jnp.zeros(q.shape, F32)
    for i in range(n_pg):
        p = jnp.exp(lgs[i] - m)
        l = l + p
        acc = acc + p * v_refs[i][...]
    num = jnp.sum(acc, axis=1, keepdims=True) + b[:, 0:1] * vn_ref[...][:, 0:1]
    den = jnp.sum(l, axis=1, keepdims=True) + b[:, 0:1]
    o_ref[...] = num / den


def _moba_sample(phys, start, slopes, qb, knb, vnb, cache_k, cache_v, past):
    nseq, heads, hd, _ = qb.shape
    page = cache_k.shape[-1]
    n_pg = phys.shape[0] // (nseq * heads)
    tok = pl.BlockSpec((None, None, hd, LANES), lambda s, h, ph, st: (s, h, 0, 0))

    def pg(i):
        return pl.BlockSpec((None, None, None, hd, page),
                            lambda s, h, ph, st: (0, ph[(s * heads + h) * n_pg + i], h, 0, 0))

    pages = [pg(i) for i in range(n_pg)]
    grid_spec = pltpu.PrefetchScalarGridSpec(
        num_scalar_prefetch=2,
        grid=(nseq, heads),
        in_specs=[pl.BlockSpec(memory_space=pltpu.SMEM), tok, tok, tok] + pages + pages,
        out_specs=pl.BlockSpec((None, None, hd, 1), lambda s, h, ph, st: (s, h, 0, 0)),
    )
    return pl.pallas_call(
        functools.partial(_moba_sample_kernel, past=past), grid_spec=grid_spec,
        out_shape=jax.ShapeDtypeStruct((nseq, heads, hd, 1), F32),
        compiler_params=_params("parallel", "arbitrary"),
        name="moba_sample",
    )(phys, start, slopes, qb, knb, vnb, *([cache_k] * n_pg), *([cache_v] * n_pg))


def _pad_rows(x, rows):
    return jnp.pad(x, ((0, rows - x.shape[0]),) + ((0, 0),) * (x.ndim - 1))


def _layer_weights(l, g_mix, w_in, b_forget, w_branch_a, w_branch_b, w_out, g_ffn, w_peer_q, peer_sub_keys,
                   peer_u, peer_v):
    d = w_in.shape[1]
    wa_w = w_branch_a.shape[1]
    n_qkv = 6 * wa_w
    w = w_in[l]
    wf = jnp.pad(w[:, n_qkv:n_qkv + N_HEADS], ((0, 0), (0, LANES - N_HEADS)))
    wfh = wf.astype(BF16)
    wfl = (wf - wfh.astype(F32)).astype(BF16)
    sk = peer_sub_keys[l]
    return dict(
        g_mix=g_mix[l].reshape(1, d), wqkv=w[:, :n_qkv].astype(BF16), wfh=wfh, wfl=wfl,
        bf=b_forget[l].reshape(1, N_HEADS), wg=w[:, n_qkv + N_HEADS:].astype(BF16),
        wa=w_branch_a[l].astype(BF16), wb=w_branch_b[l].astype(BF16), wo=w_out[l].astype(BF16),
        g_ffn=g_ffn[l].reshape(1, d), wq=w_peer_q[l].astype(BF16),
        sk=sk.reshape((sk.shape[0] * sk.shape[1],) + sk.shape[2:]).astype(BF16),
        u=peer_u[l].astype(BF16), vt=peer_v[l].T.astype(BF16))


def _ffn(x, oa, ob, ga, gb, wts, g_out, tm, tb, ecn):
    h, hn, st = _merge(x, oa, ob, ga, gb, wts["wa"], wts["wb"], wts["wo"], wts["g_ffn"], wts["wq"], wts["sk"], tm)
    rank2, f2, cnt, c1 = _select(st)
    return _peer(hn, h, rank2, f2, cnt, c1, wts["u"], wts["vt"], g_out, tb, ecn)


def kernel(x_prompt, x_sample, cache_moba_k, cache_moba_v, cache_fox_k, cache_fox_v, cache_fox_logf, page_table,
           g_mix, w_in, b_forget, w_branch_a, w_branch_b, w_out, g_ffn, w_peer_q, peer_sub_keys, peer_u, peer_v,
           g_final):
    batch, seq, d = x_prompt.shape
    nseq, dec_seq, _ = x_sample.shape
    depth = w_in.shape[0]
    assert depth == 1 and dec_seq == 1, "one layer, one new token per sampled sequence"
    npages, page = page_table.shape[1], cache_moba_k.shape[2]
    past = npages * page
    assert past % MOBA_BLOCK == 0 and past // MOBA_BLOCK >= MOBA_TOPK and seq % MOBA_BLOCK == 0
    assert page == LANES
    slopes = 2.0 ** (-8.0 * jnp.arange(1, N_HEADS + 1, dtype=F32) / N_HEADS)
    wts = _layer_weights(0, g_mix, w_in, b_forget, w_branch_a, w_branch_b, w_out, g_ffn, w_peer_q, peer_sub_keys,
                         peer_u, peer_v)
    g_out = g_final.reshape(1, d)
    proj_w = (wts["g_mix"], wts["wqkv"], wts["wfh"], wts["wfl"], wts["bf"], wts["wg"])
    npair = N_HEADS // 2
    ecn = SUBLANES * PEER_NKEYS

    n = batch * seq
    xp = x_prompt.reshape(n, d)
    qa, ka, va, qb, kb, vb, lf, ga, gb = _proj(xp, *proj_w, tm=min(256, n))
    c = _cumsum_rows(lf.reshape(batch, seq, N_HEADS).transpose(1, 0, 2).reshape(seq, batch * N_HEADS))
    c = c.reshape(seq, batch, npair, 2)
    oa = _moba_prompt(slopes, qa, ka, va, batch, seq)
    ob = _fox_prompt(qb, kb, vb, c.transpose(1, 2, 3, 0), c.transpose(1, 2, 0, 3), batch, seq)
    y_prompt = _ffn(xp, oa, ob, ga, gb, wts, g_out, min(256, n), min(512, n), ecn).reshape(batch, seq, d)
    kv_shape = (1, batch, seq, N_HEADS, HEAD_DIM)
    p_new = (ka.reshape(kv_shape), va.reshape(kv_shape), kb.reshape(kv_shape), vb.reshape(kv_shape),
             lf.reshape(1, batch, seq, N_HEADS))

    ns = -(-nseq // LANES) * LANES
    xs = _pad_rows(x_sample.reshape(nseq, d), ns)
    qa, ka, va, qb, kb, vb, lf, ga, gb = _proj(xs, *proj_w, tm=ns)
    lane_bcast = lambda z: jnp.broadcast_to(z[:nseq].reshape(nseq, N_HEADS, HEAD_DIM, 1),
                                            (nseq, N_HEADS, HEAD_DIM, LANES))
    pages_minor = lambda cache: cache.transpose(0, 1, 3, 4, 2)
    mk, mv = pages_minor(cache_moba_k), pages_minor(cache_moba_v)
    sel = _moba_gate(page_table, lane_bcast(qa), mk)[..., 0]
    ppb = MOBA_BLOCK // page
    sel_pages = sel.transpose(0, 2, 1)[..., None] * ppb + jnp.arange(ppb, dtype=jnp.int32)
    sel_pages = sel_pages.reshape(nseq, N_HEADS * MOBA_TOPK * ppb)
    phys = jnp.take_along_axis(page_table, sel_pages, axis=1).reshape(-1)
    start = (sel_pages * page).reshape(-1)
    oa = _moba_sample(phys, start, slopes, lane_bcast(qa), lane_bcast(ka), lane_bcast(va), mk, mv, past)
    lf_past = _gather_pages(page_table, cache_fox_logf.transpose(0, 1, 3, 2))
    lf_past = lf_past.transpose(0, 1, 3, 2).reshape(nseq, past, N_HEADS)
    lf_all = jnp.concatenate([lf_past, lf[:nseq, None, :]], axis=1)
    tot = -(-(past + 1) // MOBA_BLOCK) * MOBA_BLOCK
    lf_all = jnp.pad(lf_all, ((0, 0), (0, tot - past - 1), (0, 0)))
    cs = _cumsum_rows(lf_all.transpose(1, 0, 2).reshape(tot, nseq * N_HEADS)).reshape(tot, nseq, N_HEADS)
    cq = jnp.broadcast_to(cs[past][:, :, None], (nseq, N_HEADS, LANES))
    ck = cs[:past].reshape(npages, page, nseq, N_HEADS).transpose(2, 0, 3, 1)
    ob = _fox_sample(page_table, lane_bcast(qb), lane_bcast(kb), lane_bcast(vb), cq, ck,
                     pages_minor(cache_fox_k), pages_minor(cache_fox_v))
    oa = _pad_rows(oa.reshape(nseq, N_HEADS * HEAD_DIM), ns)
    ob = _pad_rows(ob.reshape(nseq, N_HEADS * HEAD_DIM), ns)
    y_sample = _ffn(xs, oa, ob, ga, gb, wts, g_out, ns, ns, ecn)[:nseq].reshape(nseq, 1, d)
    kv_shape = (1, nseq, 1, N_HEADS, HEAD_DIM)
    s_new = (ka[:nseq].reshape(kv_shape), va[:nseq].reshape(kv_shape), kb[:nseq].reshape(kv_shape),
             vb[:nseq].reshape(kv_shape), lf[:nseq].reshape(1, nseq, 1, N_HEADS))
    return (y_prompt, y_sample) + p_new + s_new
```

```python
import functools

import jax
import jax.numpy as jnp
from jax import lax
from jax.experimental import pallas as pl
from jax.experimental.pallas import tpu as pltpu

F32 = jnp.float32
BF16 = jnp.bfloat16

HEAD_DIM = 64
N_HEADS = 8
PAIR = 2 * HEAD_DIM
MOBA_BLOCK = 256
MOBA_TOPK = 3
PEER_HEADS = 8
PEER_NKEYS = 128
PEER_TOPK = 16
RMS_EPS = 1e-6
NEG = -1e30
LANES = 128
SUBLANES = 8
BF16_ROWS = 2 * SUBLANES
VMEM_LIMIT = 56 * 1024 * 1024

_NT = (((1,), (1,)), ((), ()))


def _dot(a, b):
    return jnp.dot(a, b, preferred_element_type=F32)


def _dot_nt(a, b, precision=None):
    return lax.dot_general(a, b, _NT, precision=precision, preferred_element_type=F32)


def _params(*sem):
    return pltpu.CompilerParams(dimension_semantics=sem, vmem_limit_bytes=VMEM_LIMIT)


def _half(idx):
    return jnp.where(idx >= HEAD_DIM, 1, 0)


def _rms(x, g):
    return x * lax.rsqrt(jnp.mean(x * x, axis=-1, keepdims=True) + RMS_EPS) * g


def _proj_kernel(x_ref, g_ref, wqkv_ref, wfh_ref, wfl_ref, bf_ref, wg_ref,
                 qa_ref, ka_ref, va_ref, qb_ref, kb_ref, vb_ref, lf_ref, ga_ref, gb_ref):
    xn = _rms(x_ref[...], g_ref[...])
    xb = xn.astype(BF16)
    width = qa_ref.shape[-1]
    for i, o_ref in enumerate((qa_ref, ka_ref, va_ref, qb_ref, kb_ref, vb_ref)):
        o_ref[...] = _dot(xb, wqkv_ref[:, i * width:(i + 1) * width])
    xl = (xn - xb.astype(F32)).astype(BF16)
    zf = _dot(xb, wfh_ref[...]) + (_dot(xl, wfh_ref[...]) + _dot(xb, wfl_ref[...]))
    zf = zf[:, :N_HEADS] + bf_ref[...]
    lf_ref[...] = jnp.minimum(zf, 0.0) - jnp.log1p(jnp.exp(-jnp.abs(zf)))
    d = ga_ref.shape[-1]
    ga_ref[...] = 1.0 / (1.0 + jnp.exp(-_dot(xb, wg_ref[:, :d])))
    gb_ref[...] = 1.0 / (1.0 + jnp.exp(-_dot(xb, wg_ref[:, d:])))


def _proj(x, g, wqkv, wfh, wfl, bfg, wg, tm):
    n, d = x.shape
    width = wqkv.shape[1] // 6
    row = lambda i: (i, 0)
    fix = lambda i: (0, 0)
    outs = [jax.ShapeDtypeStruct((n, width), F32)] * 6 + [
        jax.ShapeDtypeStruct((n, N_HEADS), F32),
        jax.ShapeDtypeStruct((n, d), F32), jax.ShapeDtypeStruct((n, d), F32)]
    return pl.pallas_call(
        _proj_kernel,
        grid=(n // tm,),
        in_specs=[pl.BlockSpec((tm, d), row), pl.BlockSpec((1, d), fix),
                  pl.BlockSpec(wqkv.shape, fix), pl.BlockSpec(wfh.shape, fix),
                  pl.BlockSpec(wfl.shape, fix), pl.BlockSpec((1, N_HEADS), fix),
                  pl.BlockSpec(wg.shape, fix)],
        out_specs=[pl.BlockSpec((tm, width), row)] * 6 + [
            pl.BlockSpec((tm, N_HEADS), row), pl.BlockSpec((tm, d), row), pl.BlockSpec((tm, d), row)],
        out_shape=outs,
        compiler_params=_params("parallel"),
        name="proj",
    )(x, g, wqkv, wfh, wfl, bfg, wg)


def _cumsum_kernel(x_ref, o_ref, carry_ref):
    @pl.when(pl.program_id(0) == 0)
    def _():
        carry_ref[...] = jnp.zeros_like(carry_ref)

    blk = x_ref.shape[0]
    r = lax.broadcasted_iota(jnp.int32, (blk, blk), 0)
    c = lax.broadcasted_iota(jnp.int32, (blk, blk), 1)
    tri = jnp.where(c <= r, 1.0, 0.0).astype(F32)
    y = jnp.dot(tri, x_ref[...], precision=lax.Precision.HIGHEST,
                preferred_element_type=F32) + carry_ref[...]
    o_ref[...] = y
    carry_ref[...] = y[blk - 1:blk, :]


def _cumsum_rows(x, blk=256):
    n, c = x.shape
    return pl.pallas_call(
        _cumsum_kernel,
        grid=(n // blk,),
        in_specs=[pl.BlockSpec((blk, c), lambda i: (i, 0))],
        out_specs=pl.BlockSpec((blk, c), lambda i: (i, 0)),
        out_shape=jax.ShapeDtypeStruct((n, c), F32),
        scratch_shapes=[pltpu.VMEM((1, c), F32)],
        compiler_params=_params("arbitrary"),
        name="cumsum",
    )(x)


def _stage_kv(k_ref, v_ref, kb_s, vt_s, kmean_s):
    nblk = k_ref.shape[0] // MOBA_BLOCK
    for c in range(nblk):
        rows = slice(c * MOBA_BLOCK, (c + 1) * MOBA_BLOCK)
        kc = k_ref[rows, :]
        kb_s[rows, :] = kc.astype(BF16)
        if kmean_s is not None:
            kmean_s[c:c + 1, :] = jnp.sum(kc, axis=0, keepdims=True) * (1.0 / MOBA_BLOCK)
        vt_s[:, rows] = v_ref[rows, :].T.astype(BF16)


BIG = 1e30
LOG2E = 1.4426950408889634
PAST, OWN, DEAD = 0, 1, 2


def _block_rows(blk):
    return pl.ds(pl.multiple_of(blk * MOBA_BLOCK, MOBA_BLOCK), MOBA_BLOCK)


def _flash_sweep(own, qs, kb_s, vt_s, bias_s, row_fn, col_fn, qk_s, p_s, acc_s, o_ref):
    nhq = len(qs[0])
    chains = [(e, hq) for e in range(2) for hq in range(nhq)]
    rows = _block_rows
    block_of = lambda t: jnp.where(t == 0, own, jnp.maximum(jnp.minimum(t, own) - 1, 0))

    def issue_qk(blk, slot):
        kt = kb_s[rows(blk), :]
        for c, (e, hq) in enumerate(chains):
            qk_s[slot, c] = _dot_nt(kt, qs[e][hq])

    def issue_pv(blk, slot, alphas):
        for c, (e, hq) in enumerate(chains):
            acc_s[c] = alphas[c] * acc_s[c] + _dot(vt_s[_head_rows(e), rows(blk)], p_s[slot, c])

    acc_s[...] = jnp.zeros_like(acc_s)
    p_s[1] = jnp.zeros(p_s.shape[1:], p_s.dtype)
    issue_qk(own, 0)
    row = lambda v: jnp.full((1, LANES), v, F32)
    init = (tuple(row(NEG) for _ in chains), tuple(row(0.0) for _ in chains), tuple(row(1.0) for _ in chains))

    def trip(t, slot, carry):
        ms, ls, alphas = carry
        issue_pv(block_of(jnp.maximum(t - 1, 0)), 1 - slot, alphas)
        issue_qk(block_of(t + 1), 1 - slot)
        blk = block_of(t)
        kind = jnp.where(t > own, DEAD, jnp.where(t == 0, OWN, PAST))
        off = (own - blk) * MOBA_BLOCK
        cols = [None if col_fn is None else col_fn(e, blk) for e in range(2)]
        new_m, new_l, new_a = [], [], []
        for c, (e, hq) in enumerate(chains):
            lg = (qk_s[slot, c] - bias_s[kind, c]) + row_fn(e, hq, blk, off)
            if cols[e] is not None:
                lg = lg - cols[e]
            mn = jnp.maximum(ms[c], jnp.max(lg, axis=0, keepdims=True))
            alpha = jnp.exp2(ms[c] - mn)
            p = jnp.exp2(lg - mn)
            p_s[slot, c] = p.astype(BF16)
            new_m.append(mn)
            new_l.append(alpha * ls[c] + jnp.sum(p, axis=0, keepdims=True))
            new_a.append(alpha)
        return tuple(new_m), tuple(new_l), tuple(new_a)

    n_pairs = own // 2 + 1
    ms, ls, alphas = lax.fori_loop(0, n_pairs, lambda g, c: trip(2 * g + 1, 1, trip(2 * g, 0, c)), init)
    issue_pv(block_of(2 * n_pairs - 1), 1, alphas)
    heads = [jnp.concatenate([acc_s[e * nhq + hq] / ls[e * nhq + hq] for hq in range(nhq)], axis=1)
             for e in range(2)]
    o_ref[...] = jnp.concatenate(heads, axis=0).T


def _head_rows(e):
    return slice(e * HEAD_DIM, (e + 1) * HEAD_DIM)


def _lane_half(hq):
    return slice(hq * LANES, (hq + 1) * LANES)


def _sweep_scratch(tq):
    nchain = 2 * (tq // LANES)
    return [pltpu.VMEM((3, nchain, MOBA_BLOCK, LANES), F32),
            pltpu.VMEM((2, nchain, MOBA_BLOCK, LANES), F32), pltpu.VMEM((2, nchain, MOBA_BLOCK, LANES), BF16),
            pltpu.VMEM((nchain, HEAD_DIM, LANES), F32)]


def _rel_pos(hq):
    return (lax.broadcasted_iota(jnp.int32, (MOBA_BLOCK, LANES), 1) + hq * LANES
            - lax.broadcasted_iota(jnp.int32, (MOBA_BLOCK, LANES), 0))


def _moba_kernel(slopes_ref, q_ref, k_ref, v_ref, o_ref, kb_s, vt_s, kmean_s, sel_s, bias_s, qk_s, p_s, acc_s):
    hp = pl.program_id(1)
    own = pl.program_id(2)
    tq = q_ref.shape[0]
    nblk = kmean_s.shape[0]
    nhq = tq // LANES
    slopes2 = [slopes_ref[hp * 2 + e] * LOG2E for e in range(2)]

    @pl.when(own == 0)
    def _():
        _stage_kv(k_ref, v_ref, kb_s, vt_s, kmean_s)
        for e in range(2):
            for hq in range(nhq):
                rel = _rel_pos(hq)
                alibi = slopes2[e] * rel.astype(F32)
                bias_s[PAST, e * nhq + hq] = alibi
                bias_s[OWN, e * nhq + hq] = jnp.where(rel >= 0, alibi, BIG)
                bias_s[DEAD, e * nhq + hq] = jnp.full(alibi.shape, BIG, F32)

    q = q_ref[...]
    lane_head = _half(lax.broadcasted_iota(jnp.int32, (1, PAIR), 1))
    blk = lax.broadcasted_iota(jnp.int32, (nblk, tq), 0)
    qs = []
    for e in range(2):
        qm = jnp.where(lane_head == e, q, 0.0)
        gate = _dot_nt(kmean_s[...], qm, precision=lax.Precision.HIGHEST)
        valid = blk < own
        gate = jnp.where(valid, gate, -jnp.inf)
        rank = jnp.zeros((nblk, tq), F32)
        for mth in range(nblk):
            g_m = gate[mth:mth + 1, :]
            beats = jnp.where(g_m > gate, 1.0, jnp.where((g_m == gate) & (blk > mth), 1.0, 0.0))
            rank = rank + beats
        sel = jnp.where((valid & (rank < MOBA_TOPK)) | (blk == own), 1.0, 0.0)
        for mth in range(nblk):
            sel_s[e, mth] = jnp.broadcast_to(sel[mth:mth + 1, :], sel_s.shape[2:])
        qe = (qm * (HEAD_DIM ** -0.5 * LOG2E)).astype(BF16)
        qs.append([qe[_lane_half(hq), :] for hq in range(nhq)])

    def row_fn(e, hq, b, off):
        picked = sel_s[e, b][0:1, _lane_half(hq)]
        return (picked - 1.0) * BIG - slopes2[e] * off.astype(F32)

    _flash_sweep(own, qs, kb_s, vt_s, bias_s, row_fn, None, qk_s, p_s, acc_s, o_ref)


def _moba_prompt(slopes, q, k, v, batch, seq):
    n, width = q.shape
    tq = MOBA_BLOCK
    nq = seq // tq
    npair = width // PAIR
    return pl.pallas_call(
        _moba_kernel,
        grid=(batch, npair, nq),
        in_specs=[pl.BlockSpec(memory_space=pltpu.SMEM),
                  pl.BlockSpec((tq, PAIR), lambda b, h, i: (b * nq + i, h)),
                  pl.BlockSpec((seq, PAIR), lambda b, h, i: (b, h)),
                  pl.BlockSpec((seq, PAIR), lambda b, h, i: (b, h))],
        out_specs=pl.BlockSpec((tq, PAIR), lambda b, h, i: (b * nq + i, h)),
        out_shape=jax.ShapeDtypeStruct((n, width), F32),
        scratch_shapes=[pltpu.VMEM((seq, PAIR), BF16), pltpu.VMEM((PAIR, seq), BF16),
                        pltpu.VMEM((seq // MOBA_BLOCK, PAIR), F32),
                        pltpu.VMEM((2, seq // MOBA_BLOCK, SUBLANES, tq), F32)] + _sweep_scratch(tq),
        compiler_params=_params("parallel", "parallel", "arbitrary"),
        name="moba",
    )(slopes, q, k, v)


def _fox_kernel(q_ref, k_ref, v_ref, cq_ref, ck_ref, o_ref, kb_s, vt_s, ck2_s, bias_s, qk_s, p_s, acc_s):
    own = pl.program_id(2)
    tq = q_ref.shape[0]
    nhq = tq // LANES

    @pl.when(own == 0)
    def _():
        _stage_kv(k_ref, v_ref, kb_s, vt_s, None)
        ck2_s[...] = ck_ref[...] * LOG2E
        for c in range(2 * nhq):
            bias_s[PAST, c] = jnp.zeros(bias_s.shape[2:], F32)
            bias_s[OWN, c] = jnp.where(_rel_pos(c % nhq) >= 0, 0.0, BIG)
            bias_s[DEAD, c] = jnp.full(bias_s.shape[2:], BIG, F32)

    q = q_ref[...]
    lane_head = _half(lax.broadcasted_iota(jnp.int32, (1, PAIR), 1))
    qs, cq2 = [], []
    for e in range(2):
        qe = (jnp.where(lane_head == e, q, 0.0) * (HEAD_DIM ** -0.5 * LOG2E)).astype(BF16)
        qs.append([qe[_lane_half(hq), :] for hq in range(nhq)])
        cq2.append([cq_ref[e:e + 1, _lane_half(hq)] * LOG2E for hq in range(nhq)])

    _flash_sweep(own, qs, kb_s, vt_s, bias_s, lambda e, hq, b, off: cq2[e][hq],
                 lambda e, b: ck2_s[_block_rows(b), e:e + 1], qk_s, p_s, acc_s, o_ref)


def _fox_prompt(q, k, v, cq, ck, batch, seq):
    n, width = q.shape
    tq = MOBA_BLOCK
    nq = seq // tq
    npair = width // PAIR
    return pl.pallas_call(
        _fox_kernel,
        grid=(batch, npair, nq),
        in_specs=[pl.BlockSpec((tq, PAIR), lambda b, h, i: (b * nq + i, h)),
                  pl.BlockSpec((seq, PAIR), lambda b, h, i: (b, h)),
                  pl.BlockSpec((seq, PAIR), lambda b, h, i: (b, h)),
                  pl.BlockSpec((None, None, 2, tq), lambda b, h, i: (b, h, 0, i)),
                  pl.BlockSpec((None, None, seq, 2), lambda b, h, i: (b, h, 0, 0))],
        out_specs=pl.BlockSpec((tq, PAIR), lambda b, h, i: (b * nq + i, h)),
        out_shape=jax.ShapeDtypeStruct((n, width), F32),
        scratch_shapes=[pltpu.VMEM((seq, PAIR), BF16), pltpu.VMEM((PAIR, seq), BF16),
                        pltpu.VMEM((seq, 2), F32)] + _sweep_scratch(tq),
        compiler_params=_params("parallel", "parallel", "arbitrary"),
        name="fox",
    )(q, k, v, cq, ck)


def _merge_kernel(x_ref, oa_ref, ob_ref, ga_ref, gb_ref, wa_ref, wb_ref, wo_ref, g_ref, wq_ref, sk_ref,
                  h_ref, hn_ref, st_ref):
    a = _dot(oa_ref[...].astype(BF16), wa_ref[...])
    b = _dot(ob_ref[...].astype(BF16), wb_ref[...])
    mix = ga_ref[...] * a + gb_ref[...] * b
    h = x_ref[...] + _dot(mix.astype(BF16), wo_ref[...])
    h_ref[...] = h
    hn = _rms(h, g_ref[...]).astype(BF16)
    hn_ref[...] = hn
    for i in range(st_ref.shape[0]):
        qp = _dot(hn, wq_ref[:, i * PEER_NKEYS:(i + 1) * PEER_NKEYS]).astype(BF16)
        st_ref[i] = _dot_nt(sk_ref[i], qp)


def _merge(x, oa, ob, ga, gb, wa, wb, wo, g, wq, sk, tm):
    n, d = x.shape
    width = oa.shape[1]
    nsk = sk.shape[0]
    row = lambda i: (i, 0)
    fix = lambda i: (0, 0)
    return pl.pallas_call(
        _merge_kernel,
        grid=(n // tm,),
        in_specs=[pl.BlockSpec((tm, d), row), pl.BlockSpec((tm, width), row), pl.BlockSpec((tm, width), row),
                  pl.BlockSpec((tm, d), row), pl.BlockSpec((tm, d), row),
                  pl.BlockSpec(wa.shape, fix), pl.BlockSpec(wb.shape, fix), pl.BlockSpec(wo.shape, fix),
                  pl.BlockSpec((1, d), fix), pl.BlockSpec(wq.shape, fix),
                  pl.BlockSpec(sk.shape, lambda i: (0, 0, 0))],
        out_specs=[pl.BlockSpec((tm, d), row), pl.BlockSpec((tm, d), row),
                   pl.BlockSpec((nsk, PEER_NKEYS, tm), lambda i: (0, 0, i))],
        out_shape=[jax.ShapeDtypeStruct((n, d), F32), jax.ShapeDtypeStruct((n, d), BF16),
                   jax.ShapeDtypeStruct((nsk, PEER_NKEYS, n), F32)],
        compiler_params=_params("parallel"),
        name="merge",
    )(x, oa, ob, ga, gb, wa, wb, wo, g, wq, sk)


def _cand_rows():
    rows = []
    for j in range(PEER_TOPK):
        rows.append(PEER_TOPK // (j + 1))
    return rows


def _extract_sorted(vals, count, dst_ref=None):
    sub = lax.broadcasted_iota(jnp.int32, vals.shape, 0)
    out = []
    for r in range(count):
        mx = jnp.max(vals, axis=0, keepdims=True)
        out.append(mx)
        if dst_ref is not None:
            dst_ref[r:r + 1, :] = mx
        first = jnp.min(jnp.where(vals == mx, sub, vals.shape[0]), axis=0, keepdims=True)
        vals = jnp.where(sub == first, -jnp.inf, vals)
    return out


def _bf16_twice(x):
    bits = pltpu.bitcast(x.astype(BF16).astype(F32), jnp.uint32)
    return bits | lax.shift_right_logical(bits, jnp.uint32(16))


def _select_head(h, st_ref, rank2_ref, f2_ref, cnt_ref, c1_ref, a_s, b_s, cand_s):
    rows = _cand_rows()
    s1 = st_ref[2 * h]
    s2 = st_ref[2 * h + 1]
    _extract_sorted(s1, PEER_TOPK, a_s)
    _extract_sorted(s2, PEER_TOPK, b_s)
    a = a_s[...]
    b = b_s[...]
    cand_s[...] = jnp.full(cand_s.shape, -jnp.inf, F32)
    off = 0
    for j, lim in enumerate(rows):
        cand_s[off:off + lim, :] = a[0:lim, :] + b[j:j + 1, :]
        off += lim
    tops = _extract_sorted(cand_s[...], PEER_TOPK)
    theta = tops[PEER_TOPK - 1]
    z = jnp.ones_like(theta)
    for t in tops[1:]:
        z = z + jnp.exp(t - tops[0])
    cnt = jnp.zeros(s1.shape, F32)
    rank2 = jnp.zeros(s2.shape, F32)
    for j in range(PEER_TOPK):
        bj = b[j:j + 1, :]
        cnt = cnt + jnp.where(s1 + bj >= theta, 1.0, 0.0)
        rank2 = rank2 + jnp.where(bj > s2, 1.0, 0.0)
    cnt_ref[h] = _bf16_twice(cnt)
    rank2_ref[h] = pltpu.bitcast(rank2.astype(BF16), jnp.uint32)
    f2_ref[h] = pltpu.bitcast(jnp.exp(s2 - b[0:1, :]).astype(BF16), jnp.uint32)
    c1_ref[h] = _bf16_twice(jnp.exp(s1 - a[0:1, :]) / z)


def _select_kernel(st_ref, rank2_ref, f2_ref, cnt_ref, c1_ref, *scratch):
    def pair(hp, carry):
        for e in range(2):
            _select_head(2 * hp + e, st_ref, rank2_ref, f2_ref, cnt_ref, c1_ref, *scratch[3 * e:3 * e + 3])
        return carry

    lax.fori_loop(0, PEER_HEADS // 2, pair, 0)


def _select(st, tl=LANES):
    nsk, nk, n = st.shape
    spec_in = pl.BlockSpec((nsk, nk, tl), lambda i: (0, 0, i))
    spec_out = lambda rows: pl.BlockSpec((PEER_HEADS, rows, tl), lambda i: (0, 0, i))
    ncand = -(-sum(_cand_rows()) // SUBLANES) * SUBLANES
    scratch = [pltpu.VMEM((PEER_TOPK, tl), F32), pltpu.VMEM((PEER_TOPK, tl), F32), pltpu.VMEM((ncand, tl), F32)]
    shape = lambda rows: jax.ShapeDtypeStruct((PEER_HEADS, rows, n), jnp.uint32)
    return pl.pallas_call(
        _select_kernel,
        grid=(n // tl,),
        in_specs=[spec_in],
        out_specs=[spec_out(nk // 2), spec_out(nk // 2), spec_out(nk), spec_out(nk)],
        out_shape=[shape(nk // 2), shape(nk // 2), shape(nk), shape(nk)],
        scratch_shapes=scratch * 2,
        compiler_params=_params("parallel"),
        name="select",
    )(st)


def _gelu(x):
    return 0.5 * x * (1.0 + jnp.tanh(0.7978845608028654 * (x + 0.044715 * (x * x * x))))


def _peer_weights(il, tc, g0, rank2_ref, f2_ref, cnt_ref, c1_ref):
    cols = slice(tc * LANES, (tc + 1) * LANES)
    cnt, c1 = [], []
    for h in range(PEER_HEADS):
        grp = pl.ds(pl.multiple_of(h * PEER_NKEYS + g0, SUBLANES), SUBLANES)
        for ref, dst in ((cnt_ref, cnt), (c1_ref, c1)):
            row = jnp.broadcast_to(ref[grp, cols][il:il + 1, :], (SUBLANES, LANES))
            dst.append(pltpu.bitcast(row, BF16))
    slabs = []
    for r in range(PEER_NKEYS // BF16_ROWS):
        w = jnp.zeros((BF16_ROWS, LANES), BF16)
        for h in range(PEER_HEADS):
            words = slice((h * PEER_NKEYS + r * BF16_ROWS) // 2, (h * PEER_NKEYS + (r + 1) * BF16_ROWS) // 2)
            rank2 = pltpu.bitcast(rank2_ref[words, cols], BF16)
            f2 = pltpu.bitcast(f2_ref[words, cols], BF16)
            w = w + jnp.where(rank2 < cnt[h], f2 * c1[h], 0.0)
        slabs.append(w)
    return slabs


def _peer_kernel(hn_ref, h_ref, rank2_ref, f2_ref, cnt_ref, c1_ref, u_ref, vtp_ref, vtc_ref, g_ref, y_ref,
                 acc_s, s_s, pa_s, pb_s):
    ec = pl.program_id(1)
    last = pl.num_programs(1) - 1
    ecn, tb = pa_s.shape
    n_i1 = ecn // PEER_NKEYS
    n_tc = tb // LANES
    assert n_i1 == SUBLANES
    n_half = 2 if n_tc % 2 == 0 else 1
    halves = [slice(k * (tb // n_half), (k + 1) * (tb // n_half)) for k in range(n_half)]

    @pl.when(ec == 0)
    def _():
        acc_s[...] = jnp.zeros_like(acc_s)
        pb_s[...] = jnp.zeros_like(pb_s)

    def step(p_prev, p_cur):
        g0 = pl.multiple_of(ec * n_i1, SUBLANES)
        for cols in halves:
            s_s[:, cols] = _dot_nt(u_ref[...], hn_ref[cols, :])
        acc_s[...] += _dot(vtp_ref[...], p_prev[...])
        for tc in range(n_tc):
            cols = slice(tc * LANES, (tc + 1) * LANES)
            for il in range(n_i1):
                for r, w in enumerate(_peer_weights(il, tc, g0, rank2_ref, f2_ref, cnt_ref, c1_ref)):
                    rows = slice(il * PEER_NKEYS + r * BF16_ROWS, il * PEER_NKEYS + (r + 1) * BF16_ROWS)
                    p_cur[rows, cols] = w * _gelu(s_s[rows, cols]).astype(BF16)

        @pl.when(ec == last)
        def _():
            y_ref[...] = _rms(h_ref[...] + (acc_s[...] + _dot(vtc_ref[...], p_cur[...])).T, g_ref[...])

    pl.when(ec % 2 == 0)(lambda: step(pb_s, pa_s))
    pl.when(ec % 2 == 1)(lambda: step(pa_s, pb_s))


def _peer(hn, h, rank2, f2, cnt, c1, u, vt, g, tb, ecn):
    n, d = h.shape
    nchunk = u.shape[0] // ecn
    assert nchunk >= 2
    tok = lambda t, e: (t, 0)
    tables = [z.reshape(-1, n) for z in (rank2, f2, cnt, c1)]
    sel = [pl.BlockSpec((z.shape[0], tb), lambda t, e: (0, t)) for z in tables]
    rank2, f2, cnt, c1 = tables
    return pl.pallas_call(
        _peer_kernel,
        grid=(n // tb, nchunk),
        in_specs=[pl.BlockSpec((tb, d), tok), pl.BlockSpec((tb, d), tok)] + sel + [
                  pl.BlockSpec((ecn, d), lambda t, e: (e, 0)),
                  pl.BlockSpec((d, ecn), lambda t, e: (0, jnp.maximum(e - 1, 0))),
                  pl.BlockSpec((d, ecn), lambda t, e: (0, (e // (nchunk - 1)) * (nchunk - 1))),
                  pl.BlockSpec((1, d), lambda t, e: (0, 0))],
        out_specs=pl.BlockSpec((tb, d), tok),
        out_shape=jax.ShapeDtypeStruct((n, d), F32),
        scratch_shapes=[pltpu.VMEM((d, tb), F32), pltpu.VMEM((ecn, tb), F32),
                        pltpu.VMEM((ecn, tb), BF16), pltpu.VMEM((ecn, tb), BF16)],
        compiler_params=_params("parallel", "arbitrary"),
        name="peer",
    )(hn, h, rank2, f2, cnt, c1, u, vt, vt, g)


def _page_index(per_step, i):
    def index(s, p, pt):
        return (0, pt[s, p * per_step + i], 0, 0, 0)
    return index


def _gather_pages_kernel(pt_ref, *refs):
    o_ref = refs[-1]
    for i in range(len(refs) - 1):
        o_ref[i] = refs[i][...]


def _gather_pages(page_table, cache_t, per_step=32):
    nseq, npages = page_table.shape
    heads, page = cache_t.shape[2:]
    per_step = min(per_step, npages)
    grid_spec = pltpu.PrefetchScalarGridSpec(
        num_scalar_prefetch=1,
        grid=(nseq, npages // per_step),
        in_specs=[pl.BlockSpec((None, None, heads, page),
                               (lambda i: lambda s, p, pt: (0, pt[s, p * per_step + i], 0, 0))(i))
                  for i in range(per_step)],
        out_specs=pl.BlockSpec((None, per_step, heads, page), lambda s, p, pt: (s, p, 0, 0)),
    )
    return pl.pallas_call(
        _gather_pages_kernel, grid_spec=grid_spec,
        out_shape=jax.ShapeDtypeStruct((nseq, npages, heads, page), F32),
        compiler_params=_params("parallel", "arbitrary"),
        name="gather_logf",
    )(page_table, *([cache_t] * per_step))


def _fox_sample_kernel(pt_ref, q_ref, kn_ref, vn_ref, cq_ref, ck_ref, *refs):
    n_pg = (len(refs) - 4) // 2
    k_refs = refs[:n_pg]
    v_refs = refs[n_pg:2 * n_pg]
    o_ref, m_s, l_s, acc_s = refs[2 * n_pg:]
    step = pl.program_id(1)
    heads = q_ref.shape[0]

    @pl.when(step == 0)
    def _():
        m_s[...] = jnp.full(m_s.shape, NEG, F32)
        l_s[...] = jnp.zeros_like(l_s)
        acc_s[...] = jnp.zeros_like(acc_s)

    for h in range(heads):
        q = q_ref[h] * HEAD_DIM ** -0.5
        bias_q = cq_ref[h:h + 1, :]
        lgs = [jnp.sum(q * k_refs[i][h], axis=0, keepdims=True) + (bias_q - ck_ref[i, h:h + 1, :])
               for i in range(n_pg)]
        top = lgs[0]
        for lg in lgs[1:]:
            top = jnp.maximum(top, lg)
        m = m_s[h:h + 1, :]
        mn = jnp.maximum(m, jnp.max(top, axis=1, keepdims=True))
        alpha = jnp.exp(m - mn)
        l = alpha * l_s[h:h + 1, :]
        acc = alpha * acc_s[h]
        for i in range(n_pg):
            p = jnp.exp(lgs[i] - mn)
            l = l + p
            acc = acc + p * v_refs[i][h]
        m_s[h:h + 1, :] = mn
        l_s[h:h + 1, :] = l
        acc_s[h] = acc

    @pl.when(step == pl.num_programs(1) - 1)
    def _():
        for h in range(heads):
            q = q_ref[h] * HEAD_DIM ** -0.5
            lg_new = jnp.sum(q * kn_ref[h], axis=0, keepdims=True)
            m = m_s[h:h + 1, :]
            mf = jnp.maximum(m, lg_new)
            a = jnp.exp(m - mf)[:, 0:1]
            b = jnp.exp(lg_new - mf)[:, 0:1]
            num = a * jnp.sum(acc_s[h], axis=1, keepdims=True) + b * vn_ref[h][:, 0:1]
            den = a * jnp.sum(l_s[h:h + 1, :], axis=1, keepdims=True) + b
            o_ref[h] = num / den


def _fox_sample(page_table, qb, knb, vnb, cq, ck, cache_k, cache_v, per_step=16):
    nseq, npages = page_table.shape
    heads, hd, page = cache_k.shape[2:]
    per_step = min(per_step, npages)
    tok = pl.BlockSpec((None, heads, hd, LANES), lambda s, p, pt: (s, 0, 0, 0))
    pages = [pl.BlockSpec((None, None, heads, hd, page), _page_index(per_step, i)) for i in range(per_step)]
    grid_spec = pltpu.PrefetchScalarGridSpec(
        num_scalar_prefetch=1,
        grid=(nseq, npages // per_step),
        in_specs=[tok, tok, tok,
                  pl.BlockSpec((None, heads, LANES), lambda s, p, pt: (s, 0, 0)),
                  pl.BlockSpec((None, per_step, heads, page), lambda s, p, pt: (s, p, 0, 0))] + pages + pages,
        out_specs=pl.BlockSpec((None, heads, hd, 1), lambda s, p, pt: (s, 0, 0, 0)),
        scratch_shapes=[pltpu.VMEM((heads, LANES), F32), pltpu.VMEM((heads, LANES), F32),
                        pltpu.VMEM((heads, hd, LANES), F32)],
    )
    return pl.pallas_call(
        _fox_sample_kernel, grid_spec=grid_spec,
        out_shape=jax.ShapeDtypeStruct((nseq, heads, hd, 1), F32),
        compiler_params=_params("parallel", "arbitrary"),
        name="fox_sample",
    )(page_table, qb, knb, vnb, cq, ck, *([cache_k] * per_step), *([cache_v] * per_step))


def _moba_gate_kernel(pt_ref, q_ref, *refs):
    n_pg = len(refs) - 2
    k_refs = refs[:n_pg]
    sel_ref, gate_s = refs[n_pg:]
    step = pl.program_id(1)
    heads, _, page = k_refs[0].shape
    pages_per_block = MOBA_BLOCK // page
    blocks_per_step = n_pg // pages_per_block
    for h in range(heads):
        q = q_ref[h]
        for i in range(blocks_per_step):
            t = jnp.sum(q * k_refs[i * pages_per_block][h], axis=0, keepdims=True)
            for j in range(1, pages_per_block):
                t = t + jnp.sum(q * k_refs[i * pages_per_block + j][h], axis=0, keepdims=True)
            gate = jnp.sum(t, axis=1, keepdims=True) * (1.0 / MOBA_BLOCK)
            gate_s[step * blocks_per_step + i, h:h + 1, :] = jnp.broadcast_to(gate, (1, LANES))

    @pl.when(step == pl.num_programs(1) - 1)
    def _():
        g = gate_s[...]
        blk = lax.broadcasted_iota(jnp.int32, g.shape, 0)
        for r in range(MOBA_TOPK):
            mx = jnp.max(g, axis=0, keepdims=True)
            first = jnp.min(jnp.where(g == mx, blk, g.shape[0]), axis=0, keepdims=True)
            sel_ref[r] = first[0]
            g = jnp.where(blk == first, -jnp.inf, g)


def _moba_gate(page_table, qb, cache_k, per_step=16):
    nseq, npages = page_table.shape
    heads, hd, page = cache_k.shape[2:]
    per_step = min(per_step, npages)
    nblk = npages * page // MOBA_BLOCK
    grid_spec = pltpu.PrefetchScalarGridSpec(
        num_scalar_prefetch=1,
        grid=(nseq, npages // per_step),
        in_specs=[pl.BlockSpec((None, heads, hd, LANES), lambda s, p, pt: (s, 0, 0, 0))]
        + [pl.BlockSpec((None, None, heads, hd, page), _page_index(per_step, i)) for i in range(per_step)],
        out_specs=pl.BlockSpec((None, MOBA_TOPK, heads, LANES), lambda s, p, pt: (s, 0, 0, 0)),
        scratch_shapes=[pltpu.VMEM((nblk, heads, LANES), F32)],
    )
    return pl.pallas_call(
        _moba_gate_kernel, grid_spec=grid_spec,
        out_shape=jax.ShapeDtypeStruct((nseq, MOBA_TOPK, heads, LANES), jnp.int32),
        compiler_params=_params("parallel", "arbitrary"),
        name="moba_gate",
    )(page_table, qb, *([cache_k] * per_step))


def _moba_sample_kernel(phys_ref, start_ref, slopes_ref, q_ref, kn_ref, vn_ref, *refs, past):
    n_pg = (len(refs) - 1) // 2
    k_refs = refs[:n_pg]
    v_refs = refs[n_pg:2 * n_pg]
    o_ref = refs[-1]
    s_id, h = pl.program_id(0), pl.program_id(1)
    heads = pl.num_programs(1)
    q = q_ref[...] * HEAD_DIM ** -0.5
    slope = slopes_ref[h]
    lane = lax.broadcasted_iota(jnp.int32, (1, k_refs[0].shape[-1]), 1)
    lg_new = jnp.sum(q * kn_ref[...], axis=0, keepdims=True)
    lgs = []
    top = None
    for i in range(n_pg):
        start = start_ref[(s_id * heads + h) * n_pg + i]
        dist = (past - start - lane).astype(F32)
        lg = jnp.sum(q * k_refs[i][...], axis=0, keepdims=True) - slope * dist
        lgs.append(lg)
        top = lg if top is None else jnp.maximum(top, lg)
    m = jnp.maximum(lg_new, jnp.max(top, axis=1, keepdims=True))
    b = jnp.exp(lg_new - m)
    l = jnp.zeros_like(top)
    acc = jnp.zeros(q.shape, F32)
    for i in range(n_pg):
        p = jnp.exp(lgs[i] - m)
        l = l + p
        acc = acc + p * v_refs[i][...]
    num = jnp.sum(acc, axis=1, keepdims=True) + b[:, 0:1] * vn_ref[...][:, 0:1]
    den = jnp.sum(l, axis=1, keepdims=True) + b[:, 0:1]
    o_ref[...] = num / den


def _moba_sample(phys, start, slopes, qb, knb, vnb, cache_k, cache_v, past):
    nseq, heads, hd, _ = qb.shape
    page = cache_k.shape[-1]
    n_pg = phys.shape[0] // (nseq * heads)
    tok = pl.BlockSpec((None, None, hd, LANES), lambda s, h, ph, st: (s, h, 0, 0))

    def pg(i):
        return pl.BlockSpec((None, None, None, hd, page),
                            lambda s, h, ph, st: (0, ph[(s * heads + h) * n_pg + i], h, 0, 0))

    pages = [pg(i) for i in range(n_pg)]
    grid_spec = pltpu.PrefetchScalarGridSpec(
        num_scalar_prefetch=2,
        grid=(nseq, heads),
        in_specs=[pl.BlockSpec(memory_space=pltpu.SMEM), tok, tok, tok] + pages + pages,
        out_specs=pl.BlockSpec((None, None, hd, 1), lambda s, h, ph, st: (s, h, 0, 0)),
    )
    return pl.pallas_call(
        functools.partial(_moba_sample_kernel, past=past), grid_spec=grid_spec,
        out_shape=jax.ShapeDtypeStruct((nseq, heads, hd, 1), F32),
        compiler_params=_params("parallel", "arbitrary"),
        name="moba_sample",
    )(phys, start, slopes, qb, knb, vnb, *([cache_k] * n_pg), *([cache_v] * n_pg))


def _pad_rows(x, rows):
    return jnp.pad(x, ((0, rows - x.shape[0]),) + ((0, 0),) * (x.ndim - 1))


def _layer_weights(l, g_mix, w_in, b_forget, w_branch_a, w_branch_b, w_out, g_ffn, w_peer_q, peer_sub_keys,
                   peer_u, peer_v):
    d = w_in.shape[1]
    wa_w = w_branch_a.shape[1]
    n_qkv = 6 * wa_w
    w = w_in[l]
    wf = jnp.pad(w[:, n_qkv:n_qkv + N_HEADS], ((0, 0), (0, LANES - N_HEADS)))
    wfh = wf.astype(BF16)
    wfl = (wf - wfh.astype(F32)).astype(BF16)
    sk = peer_sub_keys[l]
    return dict(
        g_mix=g_mix[l].reshape(1, d), wqkv=w[:, :n_qkv].astype(BF16), wfh=wfh, wfl=wfl,
        bf=b_forget[l].reshape(1, N_HEADS), wg=w[:, n_qkv + N_HEADS:].astype(BF16),
        wa=w_branch_a[l].astype(BF16), wb=w_branch_b[l].astype(BF16), wo=w_out[l].astype(BF16),
        g_ffn=g_ffn[l].reshape(1, d), wq=w_peer_q[l].astype(BF16),
        sk=sk.reshape((sk.shape[0] * sk.shape[1],) + sk.shape[2:]).astype(BF16),
        u=peer_u[l].astype(BF16), vt=peer_v[l].T.astype(BF16))


def _ffn(x, oa, ob, ga, gb, wts, g_out, tm, tb, ecn):
    h, hn, st = _merge(x, oa, ob, ga, gb, wts["wa"], wts["wb"], wts["wo"], wts["g_ffn"], wts["wq"], wts["sk"], tm)
    rank2, f2, cnt, c1 = _select(st)
    return _peer(hn, h, rank2, f2, cnt, c1, wts["u"], wts["vt"], g_out, tb, ecn)


def kernel(x_prompt, x_sample, cache_moba_k, cache_moba_v, cache_fox_k, cache_fox_v, cache_fox_logf, page_table,
           g_mix, w_in, b_forget, w_branch_a, w_branch_b, w_out, g_ffn, w_peer_q, peer_sub_keys, peer_u, peer_v,
           g_final):
    batch, seq, d = x_prompt.shape
    nseq, dec_seq, _ = x_sample.shape
    depth = w_in.shape[0]
    assert depth == 1 and dec_seq == 1, "one layer, one new token per sampled sequence"
    npages, page = page_table.shape[1], cache_moba_k.shape[2]
    past = npages * page
    assert past % MOBA_BLOCK == 0 and past // MOBA_BLOCK >= MOBA_TOPK and seq % MOBA_BLOCK == 0
    assert page == LANES
    slopes = 2.0 ** (-8.0 * jnp.arange(1, N_HEADS + 1, dtype=F32) / N_HEADS)
    wts = _layer_weights(0, g_mix, w_in, b_forget, w_branch_a, w_branch_b, w_out, g_ffn, w_peer_q, peer_sub_keys,
                         peer_u, peer_v)
    g_out = g_final.reshape(1, d)
    proj_w = (wts["g_mix"], wts["wqkv"], wts["wfh"], wts["wfl"], wts["bf"], wts["wg"])
    npair = N_HEADS // 2
    ecn = SUBLANES * PEER_NKEYS

    n = batch * seq
    xp = x_prompt.reshape(n, d)
    qa, ka, va, qb, kb, vb, lf, ga, gb = _proj(xp, *proj_w, tm=min(256, n))
    c = _cumsum_rows(lf.reshape(batch, seq, N_HEADS).transpose(1, 0, 2).reshape(seq, batch * N_HEADS))
    c = c.reshape(seq, batch, npair, 2)
    oa = _moba_prompt(slopes, qa, ka, va, batch, seq)
    ob = _fox_prompt(qb, kb, vb, c.transpose(1, 2, 3, 0), c.transpose(1, 2, 0, 3), batch, seq)
    y_prompt = _ffn(xp, oa, ob, ga, gb, wts, g_out, min(256, n), min(512, n), ecn).reshape(batch, seq, d)
    kv_shape = (1, batch, seq, N_HEADS, HEAD_DIM)
    p_new = (ka.reshape(kv_shape), va.reshape(kv_shape), kb.reshape(kv_shape), vb.reshape(kv_shape),
             lf.reshape(1, batch, seq, N_HEADS))

    ns = -(-nseq // LANES) * LANES
    xs = _pad_rows(x_sample.reshape(nseq, d), ns)
    qa, ka, va, qb, kb, vb, lf, ga, gb = _proj(xs, *proj_w, tm=ns)
    lane_bcast = lambda z: jnp.broadcast_to(z[:nseq].reshape(nseq, N_HEADS, HEAD_DIM, 1),
                                            (nseq, N_HEADS, HEAD_DIM, LANES))
    pages_minor = lambda cache: cache.transpose(0, 1, 3, 4, 2)
    mk, mv = pages_minor(cache_moba_k), pages_minor(cache_moba_v)
    sel = _moba_gate(page_table, lane_bcast(qa), mk)[..., 0]
    ppb = MOBA_BLOCK // page
    sel_pages = sel.transpose(0, 2, 1)[..., None] * ppb + jnp.arange(ppb, dtype=jnp.int32)
    sel_pages = sel_pages.reshape(nseq, N_HEADS * MOBA_TOPK * ppb)
    phys = jnp.take_along_axis(page_table, sel_pages, axis=1).reshape(-1)
    start = (sel_pages * page).reshape(-1)
    oa = _moba_sample(phys, start, slopes, lane_bcast(qa), lane_bcast(ka), lane_bcast(va), mk, mv, past)
    lf_past = _gather_pages(page_table, cache_fox_logf.transpose(0, 1, 3, 2))
    lf_past = lf_past.transpose(0, 1, 3, 2).reshape(nseq, past, N_HEADS)
    lf_all = jnp.concatenate([lf_past, lf[:nseq, None, :]], axis=1)
    tot = -(-(past + 1) // MOBA_BLOCK) * MOBA_BLOCK
    lf_all = jnp.pad(lf_all, ((0, 0), (0, tot - past - 1), (0, 0)))
    cs = _cumsum_rows(lf_all.transpose(1, 0, 2).reshape(tot, nseq * N_HEADS)).reshape(tot, nseq, N_HEADS)
    cq = jnp.broadcast_to(cs[past][:, :, None], (nseq, N_HEADS, LANES))
    ck = cs[:past].reshape(npages, page, nseq, N_HEADS).transpose(2, 0, 3, 1)
    ob = _fox_sample(page_table, lane_bcast(qb), lane_bcast(kb), lane_bcast(vb), cq, ck,
                     pages_minor(cache_fox_k), pages_minor(cache_fox_v))
    oa = _pad_rows(oa.reshape(nseq, N_HEADS * HEAD_DIM), ns)
    ob = _pad_rows(ob.reshape(nseq, N_HEADS * HEAD_DIM), ns)
    y_sample = _ffn(xs, oa, ob, ga, gb, wts, g_out, ns, ns, ecn)[:nseq].reshape(nseq, 1, d)
    kv_shape = (1, nseq, 1, N_HEADS, HEAD_DIM)
    s_new = (ka[:nseq].reshape(kv_shape), va[:nseq].reshape(kv_shape), kb[:nseq].reshape(kv_shape),
             vb[:nseq].reshape(kv_shape), lf[:nseq].reshape(1, nseq, 1, N_HEADS))
    return (y_prompt, y_sample) + p_new + s_new
```

```python
import functools

import jax
import jax.numpy as jnp
from jax import lax
from jax.experimental import pallas as pl
from jax.experimental.pallas import tpu as pltpu

F32 = jnp.float32
BF16 = jnp.bfloat16

HEAD_DIM = 64
N_HEADS = 8
PAIR = 2 * HEAD_DIM
MOBA_BLOCK = 256
MOBA_TOPK = 3
PEER_HEADS = 8
PEER_NKEYS = 128
PEER_TOPK = 16
RMS_EPS = 1e-6
NEG = -1e30
LANES = 128
SUBLANES = 8
BF16_ROWS = 2 * SUBLANES
VMEM_LIMIT = 56 * 1024 * 1024

_NT = (((1,), (1,)), ((), ()))


def _dot(a, b):
    return jnp.dot(a, b, preferred_element_type=F32)


def _dot_nt(a, b, precision=None):
    return lax.dot_general(a, b, _NT, precision=precision, preferred_element_type=F32)


def _params(*sem):
    return pltpu.CompilerParams(dimension_semantics=sem, vmem_limit_bytes=VMEM_LIMIT)


def _half(idx):
    return jnp.where(idx >= HEAD_DIM, 1, 0)


def _rms(x, g):
    return x * lax.rsqrt(jnp.mean(x * x, axis=-1, keepdims=True) + RMS_EPS) * g


def _proj_kernel(x_ref, g_ref, wqkv_ref, wfh_ref, wfl_ref, bf_ref, wg_ref,
                 qa_ref, ka_ref, va_ref, qb_ref, kb_ref, vb_ref, lf_ref, ga_ref, gb_ref):
    xn = _rms(x_ref[...], g_ref[...])
    xb = xn.astype(BF16)
    width = qa_ref.shape[-1]
    for i, o_ref in enumerate((qa_ref, ka_ref, va_ref, qb_ref, kb_ref, vb_ref)):
        o_ref[...] = _dot(xb, wqkv_ref[:, i * width:(i + 1) * width])
    xl = (xn - xb.astype(F32)).astype(BF16)
    zf = _dot(xb, wfh_ref[...]) + (_dot(xl, wfh_ref[...]) + _dot(xb, wfl_ref[...]))
    zf = zf[:, :N_HEADS] + bf_ref[...]
    lf_ref[...] = jnp.minimum(zf, 0.0) - jnp.log1p(jnp.exp(-jnp.abs(zf)))
    d = ga_ref.shape[-1]
    ga_ref[...] = 1.0 / (1.0 + jnp.exp(-_dot(xb, wg_ref[:, :d])))
    gb_ref[...] = 1.0 / (1.0 + jnp.exp(-_dot(xb, wg_ref[:, d:])))


def _proj(x, g, wqkv, wfh, wfl, bfg, wg, tm):
    n, d = x.shape
    width = wqkv.shape[1] // 6
    row = lambda i: (i, 0)
    fix = lambda i: (0, 0)
    outs = [jax.ShapeDtypeStruct((n, width), F32)] * 6 + [
        jax.ShapeDtypeStruct((n, N_HEADS), F32),
        jax.ShapeDtypeStruct((n, d), F32), jax.ShapeDtypeStruct((n, d), F32)]
    return pl.pallas_call(
        _proj_kernel,
        grid=(n // tm,),
        in_specs=[pl.BlockSpec((tm, d), row), pl.BlockSpec((1, d), fix),
                  pl.BlockSpec(wqkv.shape, fix), pl.BlockSpec(wfh.shape, fix),
                  pl.BlockSpec(wfl.shape, fix), pl.BlockSpec((1, N_HEADS), fix),
                  pl.BlockSpec(wg.shape, fix)],
        out_specs=[pl.BlockSpec((tm, width), row)] * 6 + [
            pl.BlockSpec((tm, N_HEADS), row), pl.BlockSpec((tm, d), row), pl.BlockSpec((tm, d), row)],
        out_shape=outs,
        compiler_params=_params("parallel"),
        name="proj",
    )(x, g, wqkv, wfh, wfl, bfg, wg)


def _cumsum_kernel(x_ref, o_ref, carry_ref):
    @pl.when(pl.program_id(0) == 0)
    def _():
        carry_ref[...] = jnp.zeros_like(carry_ref)

    blk = x_ref.shape[0]
    r = lax.broadcasted_iota(jnp.int32, (blk, blk), 0)
    c = lax.broadcasted_iota(jnp.int32, (blk, blk), 1)
    tri = jnp.where(c <= r, 1.0, 0.0).astype(F32)
    y = jnp.dot(tri, x_ref[...], precision=lax.Precision.HIGHEST,
                preferred_element_type=F32) + carry_ref[...]
    o_ref[...] = y
    carry_ref[...] = y[blk - 1:blk, :]


def _cumsum_rows(x, blk=256):
    n, c = x.shape
    return pl.pallas_call(
        _cumsum_kernel,
        grid=(n // blk,),
        in_specs=[pl.BlockSpec((blk, c), lambda i: (i, 0))],
        out_specs=pl.BlockSpec((blk, c), lambda i: (i, 0)),
        out_shape=jax.ShapeDtypeStruct((n, c), F32),
        scratch_shapes=[pltpu.VMEM((1, c), F32)],
        compiler_params=_params("arbitrary"),
        name="cumsum",
    )(x)


def _stage_kv(k_ref, v_ref, kb_s, vt_s, kmean_s):
    nblk = k_ref.shape[0] // MOBA_BLOCK
    for c in range(nblk):
        rows = slice(c * MOBA_BLOCK, (c + 1) * MOBA_BLOCK)
        kc = k_ref[rows, :]
        if kb_s is not None:
            kb_s[rows, :] = kc.astype(BF16)
        if kmean_s is not None:
            kmean_s[c:c + 1, :] = jnp.sum(kc, axis=0, keepdims=True) * (1.0 / MOBA_BLOCK)
        vt_s[:, rows] = v_ref[rows, :].T.astype(BF16)


BIG = 1e30
LOG2E = 1.4426950408889634
PAST, OWN, DEAD = 0, 1, 2


def _block_rows(blk):
    return pl.ds(pl.multiple_of(blk * MOBA_BLOCK, MOBA_BLOCK), MOBA_BLOCK)


def _flash_sweep(own, qs, kb_fn, vt_s, bias_s, row_fn, qk_s, p_s, acc_s, o_ref):
    nhq = qs[0].shape[0] // LANES
    chains = [(e, hq) for e in range(2) for hq in range(nhq)]
    rows = _block_rows
    block_of = lambda t: jnp.where(t == 0, own, jnp.maximum(jnp.minimum(t, own) - 1, 0))

    def issue_qk(blk, slot):
        for e in range(2):
            raw = _dot_nt(kb_fn(e, blk), qs[e])
            for hq in range(nhq):
                qk_s[slot, e * nhq + hq] = raw[:, _lane_half(hq)]

    def issue_pv(blk, slot, alphas):
        for e in range(2):
            pv = _dot(vt_s[_head_rows(e), rows(blk)], p_s[slot, e])
            for hq in range(nhq):
                c = e * nhq + hq
                acc_s[c] = alphas[c] * acc_s[c] + pv[:, _lane_half(hq)]

    acc_s[...] = jnp.zeros_like(acc_s)
    p_s[1] = jnp.zeros(p_s.shape[1:], p_s.dtype)
    issue_qk(own, 0)
    row = lambda v: jnp.full((1, LANES), v, F32)
    init = (tuple(row(NEG) for _ in chains), tuple(row(0.0) for _ in chains), tuple(row(1.0) for _ in chains))

    def trip(t, slot, carry):
        ms, ls, alphas = carry
        issue_pv(block_of(jnp.maximum(t - 1, 0)), 1 - slot, alphas)
        issue_qk(block_of(t + 1), 1 - slot)
        blk = block_of(t)
        kind = jnp.where(t > own, DEAD, jnp.where(t == 0, OWN, PAST))
        off = (own - blk) * MOBA_BLOCK
        new_m, new_l, new_a = [], [], []
        for c, (e, hq) in enumerate(chains):
            lg = qk_s[slot, c] - bias_s[kind, c]
            if row_fn is not None:
                lg = lg + row_fn(e, hq, blk, off)
            mn = jnp.maximum(ms[c], jnp.max(lg, axis=0, keepdims=True))
            alpha = jnp.exp2(ms[c] - mn)
            p = jnp.exp2(lg - mn)
            p_s[slot, e, :, _lane_half(hq)] = p.astype(BF16)
            new_m.append(mn)
            new_l.append(alpha * ls[c] + jnp.sum(p, axis=0, keepdims=True))
            new_a.append(alpha)
        return tuple(new_m), tuple(new_l), tuple(new_a)

    n_pairs = own // 2 + 1
    ms, ls, alphas = lax.fori_loop(0, n_pairs, lambda g, c: trip(2 * g + 1, 1, trip(2 * g, 0, c)), init)
    issue_pv(block_of(2 * n_pairs - 1), 1, alphas)
    heads = [jnp.concatenate([acc_s[e * nhq + hq] / ls[e * nhq + hq] for hq in range(nhq)], axis=1)
             for e in range(2)]
    o_ref[...] = jnp.concatenate(heads, axis=0).T


def _head_rows(e):
    return slice(e * HEAD_DIM, (e + 1) * HEAD_DIM)


def _lane_half(hq):
    return slice(hq * LANES, (hq + 1) * LANES)


def _sweep_scratch(tq):
    nchain = 2 * (tq // LANES)
    return [pltpu.VMEM((3, nchain, MOBA_BLOCK, LANES), F32),
            pltpu.VMEM((2, nchain, MOBA_BLOCK, LANES), F32), pltpu.VMEM((2, 2, MOBA_BLOCK, tq), BF16),
            pltpu.VMEM((nchain, HEAD_DIM, LANES), F32)]


def _rel_pos(hq):
    return (lax.broadcasted_iota(jnp.int32, (MOBA_BLOCK, LANES), 1) + hq * LANES
            - lax.broadcasted_iota(jnp.int32, (MOBA_BLOCK, LANES), 0))


def _moba_kernel(slopes_ref, q_ref, k_ref, v_ref, o_ref, kb_s, vt_s, kmean_s, sel_s, bias_s, qk_s, p_s, acc_s):
    hp = pl.program_id(1)
    own = pl.program_id(2)
    tq = q_ref.shape[0]
    nblk = kmean_s.shape[0]
    nhq = tq // LANES
    slopes2 = [slopes_ref[hp * 2 + e] * LOG2E for e in range(2)]

    @pl.when(own == 0)
    def _():
        _stage_kv(k_ref, v_ref, kb_s, vt_s, kmean_s)
        for e in range(2):
            for hq in range(nhq):
                rel = _rel_pos(hq)
                alibi = slopes2[e] * rel.astype(F32)
                bias_s[PAST, e * nhq + hq] = alibi
                bias_s[OWN, e * nhq + hq] = jnp.where(rel >= 0, alibi, BIG)
                bias_s[DEAD, e * nhq + hq] = jnp.full(alibi.shape, BIG, F32)

    q = q_ref[...]
    lane_head = _half(lax.broadcasted_iota(jnp.int32, (1, PAIR), 1))
    blk = lax.broadcasted_iota(jnp.int32, (nblk, tq), 0)
    qs = []
    for e in range(2):
        qm = jnp.where(lane_head == e, q, 0.0)
        gate = _dot_nt(kmean_s[...], qm, precision=lax.Precision.HIGHEST)
        valid = blk < own
        gate = jnp.where(valid, gate, -jnp.inf)
        rank = jnp.zeros((nblk, tq), F32)
        for mth in range(nblk):
            g_m = gate[mth:mth + 1, :]
            beats = jnp.where(g_m > gate, 1.0, jnp.where((g_m == gate) & (blk > mth), 1.0, 0.0))
            rank = rank + beats
        sel = jnp.where((valid & (rank < MOBA_TOPK)) | (blk == own), 1.0, 0.0)
        for mth in range(nblk):
            sel_s[e, mth] = jnp.broadcast_to(sel[mth:mth + 1, :], sel_s.shape[2:])
        qe = (qm * (HEAD_DIM ** -0.5 * LOG2E)).astype(BF16)
        qs.append(qe)

    def row_fn(e, hq, b, off):
        picked = sel_s[e, b][0:1, _lane_half(hq)]
        return (picked - 1.0) * BIG - slopes2[e] * off.astype(F32)

    _flash_sweep(own, qs, lambda e, b: kb_s[_block_rows(b), :], vt_s, bias_s, row_fn, qk_s, p_s, acc_s, o_ref)


def _moba_prompt(slopes, q, k, v, batch, seq):
    n, width = q.shape
    tq = MOBA_BLOCK
    nq = seq // tq
    npair = width // PAIR
    return pl.pallas_call(
        _moba_kernel,
        grid=(batch, npair, nq),
        in_specs=[pl.BlockSpec(memory_space=pltpu.SMEM),
                  pl.BlockSpec((tq, PAIR), lambda b, h, i: (b * nq + i, h)),
                  pl.BlockSpec((seq, PAIR), lambda b, h, i: (b, h)),
                  pl.BlockSpec((seq, PAIR), lambda b, h, i: (b, h))],
        out_specs=pl.BlockSpec((tq, PAIR), lambda b, h, i: (b * nq + i, h)),
        out_shape=jax.ShapeDtypeStruct((n, width), F32),
        scratch_shapes=[pltpu.VMEM((seq, PAIR), BF16), pltpu.VMEM((PAIR, seq), BF16),
                        pltpu.VMEM((seq // MOBA_BLOCK, PAIR), F32),
                        pltpu.VMEM((2, seq // MOBA_BLOCK, SUBLANES, tq), F32)] + _sweep_scratch(tq),
        compiler_params=_params("parallel", "parallel", "arbitrary"),
        name="moba",
    )(slopes, q, k, v)


def _split3(x):
    hi = x.astype(BF16).astype(F32)
    mid = (x - hi).astype(BF16).astype(F32)
    lo = ((x - hi) - mid).astype(BF16).astype(F32)
    return hi, mid, lo


def _with_bias_lanes(x, e, ones_at, column, column_at):
    lane = lax.broadcasted_iota(jnp.int32, (1, PAIR), 1)
    base = (1 - e) * HEAD_DIM
    out = jnp.where(_half(lane) == e, x, 0.0)
    out = jnp.where((lane >= base + ones_at) & (lane < base + ones_at + 3), 1.0, out)
    for i, part in enumerate(_split3(column)):
        out = jnp.where(lane == base + column_at + i, part, out)
    return out


def _fox_kernel(q_ref, k_ref, v_ref, c_ref, o_ref, kb_s, vt_s, bias_s, qk_s, p_s, acc_s):
    own = pl.program_id(2)
    tq = q_ref.shape[0]
    nhq = tq // LANES

    @pl.when(own == 0)
    def _():
        _stage_kv(k_ref, v_ref, None, vt_s, None)
        for blk in range(k_ref.shape[0] // MOBA_BLOCK):
            rows = slice(blk * MOBA_BLOCK, (blk + 1) * MOBA_BLOCK)
            for e in range(2):
                kb_s[e, rows, :] = _with_bias_lanes(k_ref[rows, :], e, 3, c_ref[rows, e:e + 1] * -LOG2E, 0).astype(BF16)
        for c in range(2 * nhq):
            bias_s[PAST, c] = jnp.zeros(bias_s.shape[2:], F32)
            bias_s[OWN, c] = jnp.where(_rel_pos(c % nhq) >= 0, 0.0, BIG)
            bias_s[DEAD, c] = jnp.full(bias_s.shape[2:], BIG, F32)

    q = q_ref[...] * (HEAD_DIM ** -0.5 * LOG2E)
    qs = [_with_bias_lanes(q, e, 0, c_ref[_block_rows(own), e:e + 1] * LOG2E, 3).astype(BF16) for e in range(2)]
    _flash_sweep(own, qs, lambda e, b: kb_s[e, _block_rows(b), :], vt_s, bias_s, None, qk_s, p_s, acc_s, o_ref)


def _fox_prompt(q, k, v, c, batch, seq):
    n, width = q.shape
    tq = MOBA_BLOCK
    nq = seq // tq
    npair = width // PAIR
    return pl.pallas_call(
        _fox_kernel,
        grid=(batch, npair, nq),
        in_specs=[pl.BlockSpec((tq, PAIR), lambda b, h, i: (b * nq + i, h)),
                  pl.BlockSpec((seq, PAIR), lambda b, h, i: (b, h)),
                  pl.BlockSpec((seq, PAIR), lambda b, h, i: (b, h)),
                  pl.BlockSpec((None, None, seq, 2), lambda b, h, i: (b, h, 0, 0))],
        out_specs=pl.BlockSpec((tq, PAIR), lambda b, h, i: (b * nq + i, h)),
        out_shape=jax.ShapeDtypeStruct((n, width), F32),
        scratch_shapes=[pltpu.VMEM((2, seq, PAIR), BF16), pltpu.VMEM((PAIR, seq), BF16)] + _sweep_scratch(tq),
        compiler_params=_params("parallel", "parallel", "arbitrary"),
        name="fox",
    )(q, k, v, c)


def _merge_kernel(x_ref, oa_ref, ob_ref, ga_ref, gb_ref, wa_ref, wb_ref, wo_ref, g_ref, wq_ref, sk_ref,
                  h_ref, hn_ref, st_ref):
    a = _dot(oa_ref[...].astype(BF16), wa_ref[...])
    b = _dot(ob_ref[...].astype(BF16), wb_ref[...])
    mix = ga_ref[...] * a + gb_ref[...] * b
    h = x_ref[...] + _dot(mix.astype(BF16), wo_ref[...])
    h_ref[...] = h
    hn = _rms(h, g_ref[...]).astype(BF16)
    hn_ref[...] = hn
    for i in range(0, st_ref.shape[0], 2):
        qp = _dot(hn, wq_ref[:, i * PEER_NKEYS:(i + 2) * PEER_NKEYS]).astype(BF16)
        for j in range(2):
            st_ref[i + j] = _dot_nt(sk_ref[i + j], qp[:, j * PEER_NKEYS:(j + 1) * PEER_NKEYS])


def _merge(x, oa, ob, ga, gb, wa, wb, wo, g, wq, sk, tm):
    n, d = x.shape
    width = oa.shape[1]
    nsk = sk.shape[0]
    row = lambda i: (i, 0)
    fix = lambda i: (0, 0)
    return pl.pallas_call(
        _merge_kernel,
        grid=(n // tm,),
        in_specs=[pl.BlockSpec((tm, d), row), pl.BlockSpec((tm, width), row), pl.BlockSpec((tm, width), row),
                  pl.BlockSpec((tm, d), row), pl.BlockSpec((tm, d), row),
                  pl.BlockSpec(wa.shape, fix), pl.BlockSpec(wb.shape, fix), pl.BlockSpec(wo.shape, fix),
                  pl.BlockSpec((1, d), fix), pl.BlockSpec(wq.shape, fix),
                  pl.BlockSpec(sk.shape, lambda i: (0, 0, 0))],
        out_specs=[pl.BlockSpec((tm, d), row), pl.BlockSpec((tm, d), row),
                   pl.BlockSpec((nsk, PEER_NKEYS, tm), lambda i: (0, 0, i))],
        out_shape=[jax.ShapeDtypeStruct((n, d), F32), jax.ShapeDtypeStruct((n, d), BF16),
                   jax.ShapeDtypeStruct((nsk, PEER_NKEYS, n), F32)],
        compiler_params=_params("parallel"),
        name="merge",
    )(x, oa, ob, ga, gb, wa, wb, wo, g, wq, sk)


def _cand_rows():
    rows = []
    for j in range(PEER_TOPK):
        rows.append(PEER_TOPK // (j + 1))
    return rows


def _extract_sorted(vals, count, dst_ref=None):
    sub = lax.broadcasted_iota(jnp.int32, vals.shape, 0)
    out = []
    for r in range(count):
        mx = jnp.max(vals, axis=0, keepdims=True)
        out.append(mx)
        if dst_ref is not None:
            dst_ref[r:r + 1, :] = mx
        first = jnp.min(jnp.where(vals == mx, sub, vals.shape[0]), axis=0, keepdims=True)
        vals = jnp.where(sub == first, -jnp.inf, vals)
    return out


def _bf16_twice(x):
    bits = pltpu.bitcast(x.astype(BF16).astype(F32), jnp.uint32)
    return bits | lax.shift_right_logical(bits, jnp.uint32(16))


def _select_head(h, st_ref, rank2_ref, f2_ref, cnt_ref, c1_ref, a_s, b_s, cand_s):
    rows = _cand_rows()
    s1 = st_ref[2 * h]
    s2 = st_ref[2 * h + 1]
    _extract_sorted(s1, PEER_TOPK, a_s)
    _extract_sorted(s2, PEER_TOPK, b_s)
    a = a_s[...]
    b = b_s[...]
    cand_s[...] = jnp.full(cand_s.shape, -jnp.inf, F32)
    off = 0
    for j, lim in enumerate(rows):
        cand_s[off:off + lim, :] = a[0:lim, :] + b[j:j + 1, :]
        off += lim
    tops = _extract_sorted(cand_s[...], PEER_TOPK)
    theta = tops[PEER_TOPK - 1]
    z = jnp.ones_like(theta)
    for t in tops[1:]:
        z = z + jnp.exp(t - tops[0])
    cnt = jnp.zeros(s1.shape, F32)
    rank2 = jnp.zeros(s2.shape, F32)
    for j in range(PEER_TOPK):
        bj = b[j:j + 1, :]
        cnt = cnt + jnp.where(s1 + bj >= theta, 1.0, 0.0)
        rank2 = rank2 + jnp.where(bj > s2, 1.0, 0.0)
    cnt_ref[h] = _bf16_twice(cnt)
    rank2_ref[h] = pltpu.bitcast(rank2.astype(BF16), jnp.uint32)
    f2_ref[h] = pltpu.bitcast(jnp.exp(s2 - b[0:1, :]).astype(BF16), jnp.uint32)
    c1_ref[h] = _bf16_twice(jnp.exp(s1 - a[0:1, :]) / z)


def _select_kernel(st_ref, rank2_ref, f2_ref, cnt_ref, c1_ref, *scratch):
    def pair(hp, carry):
        for e in range(2):
            _select_head(2 * hp + e, st_ref, rank2_ref, f2_ref, cnt_ref, c1_ref, *scratch[3 * e:3 * e + 3])
        return carry

    lax.fori_loop(0, PEER_HEADS // 2, pair, 0)


def _select(st, tl=LANES):
    nsk, nk, n = st.shape
    spec_in = pl.BlockSpec((nsk, nk, tl), lambda i: (0, 0, i))
    spec_out = lambda rows: pl.BlockSpec((PEER_HEADS, rows, tl), lambda i: (0, 0, i))
    ncand = -(-sum(_cand_rows()) // SUBLANES) * SUBLANES
    scratch = [pltpu.VMEM((PEER_TOPK, tl), F32), pltpu.VMEM((PEER_TOPK, tl), F32), pltpu.VMEM((ncand, tl), F32)]
    shape = lambda rows: jax.ShapeDtypeStruct((PEER_HEADS, rows, n), jnp.uint32)
    return pl.pallas_call(
        _select_kernel,
        grid=(n // tl,),
        in_specs=[spec_in],
        out_specs=[spec_out(nk // 2), spec_out(nk // 2), spec_out(nk), spec_out(nk)],
        out_shape=[shape(nk // 2), shape(nk // 2), shape(nk), shape(nk)],
        scratch_shapes=scratch * 2,
        compiler_params=_params("parallel"),
        name="select",
    )(st)


def _gelu(x):
    return 0.5 * x * (1.0 + jnp.tanh(0.7978845608028654 * (x + 0.044715 * (x * x * x))))


def _peer_weights(il, tc, g0, rank2_ref, f2_ref, cnt_ref, c1_ref):
    cols = slice(tc * LANES, (tc + 1) * LANES)
    cnt, c1 = [], []
    for h in range(PEER_HEADS):
        grp = pl.ds(pl.multiple_of(h * PEER_NKEYS + g0, SUBLANES), SUBLANES)
        for ref, dst in ((cnt_ref, cnt), (c1_ref, c1)):
            row = jnp.broadcast_to(ref[grp, cols][il:il + 1, :], (SUBLANES, LANES))
            dst.append(pltpu.bitcast(row, BF16))
    slabs = []
    for r in range(PEER_NKEYS // BF16_ROWS):
        w = jnp.zeros((BF16_ROWS, LANES), BF16)
        for h in range(PEER_HEADS):
            words = slice((h * PEER_NKEYS + r * BF16_ROWS) // 2, (h * PEER_NKEYS + (r + 1) * BF16_ROWS) // 2)
            rank2 = pltpu.bitcast(rank2_ref[words, cols], BF16)
            f2 = pltpu.bitcast(f2_ref[words, cols], BF16)
            w = w + jnp.where(rank2 < cnt[h], f2 * c1[h], 0.0)
        slabs.append(w)
    return slabs


def _peer_kernel(hn_ref, h_ref, rank2_ref, f2_ref, cnt_ref, c1_ref, u_ref, vtp_ref, vtc_ref, g_ref, y_ref,
                 acc_s, s_s, pa_s, pb_s):
    ec = pl.program_id(1)
    last = pl.num_programs(1) - 1
    ecn, tb = pa_s.shape
    n_i1 = ecn // PEER_NKEYS
    n_tc = tb // LANES
    assert n_i1 == SUBLANES
    n_half = 2 if n_tc % 2 == 0 else 1
    halves = [slice(k * (tb // n_half), (k + 1) * (tb // n_half)) for k in range(n_half)]

    @pl.when(ec == 0)
    def _():
        acc_s[...] = jnp.zeros_like(acc_s)
        pb_s[...] = jnp.zeros_like(pb_s)

    def step(p_prev, p_cur):
        g0 = pl.multiple_of(ec * n_i1, SUBLANES)
        for cols in halves:
            s_s[:, cols] = _dot_nt(u_ref[...], hn_ref[cols, :])
        acc_s[...] += _dot(vtp_ref[...], p_prev[...])
        for tc in range(n_tc):
            cols = slice(tc * LANES, (tc + 1) * LANES)
            for il in range(n_i1):
                for r, w in enumerate(_peer_weights(il, tc, g0, rank2_ref, f2_ref, cnt_ref, c1_ref)):
                    rows = slice(il * PEER_NKEYS + r * BF16_ROWS, il * PEER_NKEYS + (r + 1) * BF16_ROWS)
                    p_cur[rows, cols] = w * _gelu(s_s[rows, cols].astype(BF16))

        @pl.when(ec == last)
        def _():
            y_ref[...] = _rms(h_ref[...] + (acc_s[...] + _dot(vtc_ref[...], p_cur[...])).T, g_ref[...])

    pl.when(ec % 2 == 0)(lambda: step(pb_s, pa_s))
    pl.when(ec % 2 == 1)(lambda: step(pa_s, pb_s))


def _peer(hn, h, rank2, f2, cnt, c1, u, vt, g, tb, ecn):
    n, d = h.shape
    nchunk = u.shape[0] // ecn
    assert nchunk >= 2
    tok = lambda t, e: (t, 0)
    tables = [z.reshape(-1, n) for z in (rank2, f2, cnt, c1)]
    sel = [pl.BlockSpec((z.shape[0], tb), lambda t, e: (0, t)) for z in tables]
    rank2, f2, cnt, c1 = tables
    return pl.pallas_call(
        _peer_kernel,
        grid=(n // tb, nchunk),
        in_specs=[pl.BlockSpec((tb, d), tok), pl.BlockSpec((tb, d), tok)] + sel + [
                  pl.BlockSpec((ecn, d), lambda t, e: (e, 0)),
                  pl.BlockSpec((d, ecn), lambda t, e: (0, jnp.maximum(e - 1, 0))),
                  pl.BlockSpec((d, ecn), lambda t, e: (0, (e // (nchunk - 1)) * (nchunk - 1))),
                  pl.BlockSpec((1, d), lambda t, e: (0, 0))],
        out_specs=pl.BlockSpec((tb, d), tok),
        out_shape=jax.ShapeDtypeStruct((n, d), F32),
        scratch_shapes=[pltpu.VMEM((d, tb), F32), pltpu.VMEM((ecn, tb), F32),
                        pltpu.VMEM((ecn, tb), BF16), pltpu.VMEM((ecn, tb), BF16)],
        compiler_params=_params("parallel", "arbitrary"),
        name="peer",
    )(hn, h, rank2, f2, cnt, c1, u, vt, vt, g)


def _page_index(per_step, i):
    def index(s, p, pt):
        return (0, pt[s, p * per_step + i], 0, 0, 0)
    return index


def _gather_pages_kernel(pt_ref, *refs):
    o_ref = refs[-1]
    for i in range(len(refs) - 1):
        o_ref[i] = refs[i][...]


def _gather_pages(page_table, cache_t, per_step=32):
    nseq, npages = page_table.shape
    heads, page = cache_t.shape[2:]
    per_step = min(per_step, npages)
    grid_spec = pltpu.PrefetchScalarGridSpec(
        num_scalar_prefetch=1,
        grid=(nseq, npages // per_step),
        in_specs=[pl.BlockSpec((None, None, heads, page),
                               (lambda i: lambda s, p, pt: (0, pt[s, p * per_step + i], 0, 0))(i))
                  for i in range(per_step)],
        out_specs=pl.BlockSpec((None, per_step, heads, page), lambda s, p, pt: (s, p, 0, 0)),
    )
    return pl.pallas_call(
        _gather_pages_kernel, grid_spec=grid_spec,
        out_shape=jax.ShapeDtypeStruct((nseq, npages, heads, page), F32),
        compiler_params=_params("parallel", "arbitrary"),
        name="gather_logf",
    )(page_table, *([cache_t] * per_step))


def _fox_sample_kernel(pt_ref, q_ref, kn_ref, vn_ref, cq_ref, ck_ref, *refs):
    n_pg = (len(refs) - 4) // 2
    k_refs = refs[:n_pg]
    v_refs = refs[n_pg:2 * n_pg]
    o_ref, m_s, l_s, acc_s = refs[2 * n_pg:]
    step = pl.program_id(1)
    heads = q_ref.shape[0]

    @pl.when(step == 0)
    def _():
        m_s[...] = jnp.full(m_s.shape, NEG, F32)
        l_s[...] = jnp.zeros_like(l_s)
        acc_s[...] = jnp.zeros_like(acc_s)

    for h in range(heads):
        q = q_ref[h] * HEAD_DIM ** -0.5
        bias_q = cq_ref[h:h + 1, :]
        lgs = [jnp.sum(q * k_refs[i][h], axis=0, keepdims=True) + (bias_q - ck_ref[i, h:h + 1, :])
               for i in range(n_pg)]
        top = lgs[0]
        for lg in lgs[1:]:
            top = jnp.maximum(top, lg)
        m = m_s[h:h + 1, :]
        mn = jnp.maximum(m, jnp.max(top, axis=1, keepdims=True))
        alpha = jnp.exp(m - mn)
        l = alpha * l_s[h:h + 1, :]
        acc = alpha * acc_s[h]
        for i in range(n_pg):
            p = jnp.exp(lgs[i] - mn)
            l = l + p
            acc = acc + p * v_refs[i][h]
        m_s[h:h + 1, :] = mn
        l_s[h:h + 1, :] = l
        acc_s[h] = acc

    @pl.when(step == pl.num_programs(1) - 1)
    def _():
        for h in range(heads):
            q = q_ref[h] * HEAD_DIM ** -0.5
            lg_new = jnp.sum(q * kn_ref[h], axis=0, keepdims=True)
            m = m_s[h:h + 1, :]
            mf = jnp.maximum(m, lg_new)
            a = jnp.exp(m - mf)[:, 0:1]
            b = jnp.exp(lg_new - mf)[:, 0:1]
            num = a * jnp.sum(acc_s[h], axis=1, keepdims=True) + b * vn_ref[h][:, 0:1]
            den = a * jnp.sum(l_s[h:h + 1, :], axis=1, keepdims=True) + b
            o_ref[h] = num / den


def _fox_sample(page_table, qb, knb, vnb, cq, ck, cache_k, cache_v, per_step=16):
    nseq, npages = page_table.shape
    heads, hd, page = cache_k.shape[2:]
    per_step = min(per_step, npages)
    tok = pl.BlockSpec((None, heads, hd, LANES), lambda s, p, pt: (s, 0, 0, 0))
    pages = [pl.BlockSpec((None, None, heads, hd, page), _page_index(per_step, i)) for i in range(per_step)]
    grid_spec = pltpu.PrefetchScalarGridSpec(
        num_scalar_prefetch=1,
        grid=(nseq, npages // per_step),
        in_specs=[tok, tok, tok,
                  pl.BlockSpec((None, heads, LANES), lambda s, p, pt: (s, 0, 0)),
                  pl.BlockSpec((None, per_step, heads, page), lambda s, p, pt: (s, p, 0, 0))] + pages + pages,
        out_specs=pl.BlockSpec((None, heads, hd, 1), lambda s, p, pt: (s, 0, 0, 0)),
        scratch_shapes=[pltpu.VMEM((heads, LANES), F32), pltpu.VMEM((heads, LANES), F32),
                        pltpu.VMEM((heads, hd, LANES), F32)],
    )
    return pl.pallas_call(
        _fox_sample_kernel, grid_spec=grid_spec,
        out_shape=jax.ShapeDtypeStruct((nseq, heads, hd, 1), F32),
        compiler_params=_params("parallel", "arbitrary"),
        name="fox_sample",
    )(page_table, qb, knb, vnb, cq, ck, *([cache_k] * per_step), *([cache_v] * per_step))


def _moba_gate_kernel(pt_ref, q_ref, *refs):
    n_pg = len(refs) - 2
    k_refs = refs[:n_pg]
    sel_ref, gate_s = refs[n_pg:]
    step = pl.program_id(1)
    heads, _, page = k_refs[0].shape
    pages_per_block = MOBA_BLOCK // page
    blocks_per_step = n_pg // pages_per_block
    for h in range(heads):
        q = q_ref[h]
        for i in range(blocks_per_step):
            t = jnp.sum(q * k_refs[i * pages_per_block][h], axis=0, keepdims=True)
            for j in range(1, pages_per_block):
                t = t + jnp.sum(q * k_refs[i * pages_per_block + j][h], axis=0, keepdims=True)
            gate = jnp.sum(t, axis=1, keepdims=True) * (1.0 / MOBA_BLOCK)
            gate_s[step * blocks_per_step + i, h:h + 1, :] = jnp.broadcast_to(gate, (1, LANES))

    @pl.when(step == pl.num_programs(1) - 1)
    def _():
        g = gate_s[...]
        blk = lax.broadcasted_iota(jnp.int32, g.shape, 0)
        for r in range(MOBA_TOPK):
            mx = jnp.max(g, axis=0, keepdims=True)
            first = jnp.min(jnp.where(g == mx, blk, g.shape[0]), axis=0, keepdims=True)
            sel_ref[r] = first[0]
            g = jnp.where(blk == first, -jnp.inf, g)


def _moba_gate(page_table, qb, cache_k, per_step=16):
    nseq, npages = page_table.shape
    heads, hd, page = cache_k.shape[2:]
    per_step = min(per_step, npages)
    nblk = npages * page // MOBA_BLOCK
    grid_spec = pltpu.PrefetchScalarGridSpec(
        num_scalar_prefetch=1,
        grid=(nseq, npages // per_step),
        in_specs=[pl.BlockSpec((None, heads, hd, LANES), lambda s, p, pt: (s, 0, 0, 0))]
        + [pl.BlockSpec((None, None, heads, hd, page), _page_index(per_step, i)) for i in range(per_step)],
        out_specs=pl.BlockSpec((None, MOBA_TOPK, heads, LANES), lambda s, p, pt: (s, 0, 0, 0)),
        scratch_shapes=[pltpu.VMEM((nblk, heads, LANES), F32)],
    )
    return pl.pallas_call(
        _moba_gate_kernel, grid_spec=grid_spec,
        out_shape=jax.ShapeDtypeStruct((nseq, MOBA_TOPK, heads, LANES), jnp.int32),
        compiler_params=_params("parallel", "arbitrary"),
        name="moba_gate",
    )(page_table, qb, *([cache_k] * per_step))


def _moba_sample_kernel(phys_ref, start_ref, slopes_ref, q_ref, kn_ref, vn_ref, *refs, past):
    n_pg = (len(refs) - 1) // 2
    k_refs = refs[:n_pg]
    v_refs = refs[n_pg:2 * n_pg]
    o_ref = refs[-1]
    s_id, h = pl.program_id(0), pl.program_id(1)
    heads = pl.num_programs(1)
    q = q_ref[...] * HEAD_DIM ** -0.5
    slope = slopes_ref[h]
    lane = lax.broadcasted_iota(jnp.int32, (1, k_refs[0].shape[-1]), 1)
    lg_new = jnp.sum(q * kn_ref[...], axis=0, keepdims=True)
    lgs = []
    top = None
    for i in range(n_pg):
        start = start_ref[(s_id * heads + h) * n_pg + i]
        dist = (past - start - lane).astype(F32)
        lg = jnp.sum(q * k_refs[i][...], axis=0, keepdims=True) - slope * dist
        lgs.append(lg)
        top = lg if top is None else jnp.maximum(top, lg)
    m = jnp.maximum(lg_new, jnp.max(top, axis=1, keepdims=True))
    b = jnp.exp(lg_new - m)
    l = jnp.zeros_like(top)
    acc = jnp.zeros(q.shape, F32)
    for i in range(n_pg):
        p = jnp.exp(lgs[i] - m)
        l = l + p
        acc = acc + p * v_refs[i][...]
    num = jnp.sum(acc, axis=1, keepdims=True) + b[:, 0:1] * vn_ref[...][:, 0:1]
    den = jnp.sum(l, axis=1, keepdims=True) + b[:, 0:1]
    o_ref[...] = num / den


def _moba_sample(phys, start, slopes, qb, knb, vnb, cache_k, cache_v, past):
    nseq, heads, hd, _ = qb.shape
    page = cache_k.shape[-1]
    n_pg = phys.shape[0] // (nseq * heads)
    tok = pl.BlockSpec((None, None, hd, LANES), lambda s, h, ph, st: (s, h, 0, 0))

    def pg(i):
        return pl.BlockSpec((None, None, None, hd, page),
                            lambda s, h, ph, st: (0, ph[(s * heads + h) * n_pg + i], h, 0, 0))

    pages = [pg(i) for i in range(n_pg)]
    grid_spec = pltpu.PrefetchScalarGridSpec(
        num_scalar_prefetch=2,
        grid=(nseq, heads),
        in_specs=[pl.BlockSpec(memory_space=pltpu.SMEM), tok, tok, tok] + pages + pages,
        out_specs=pl.BlockSpec((None, None, hd, 1), lambda s, h, ph, st: (s, h, 0, 0)),
    )
    return pl.pallas_call(
        functools.partial(_moba_sample_kernel, past=past), grid_spec=grid_spec,
        out_shape=jax.ShapeDtypeStruct((nseq, heads, hd, 1), F32),
        compiler_params=_params("parallel", "arbitrary"),
        name="moba_sample",
    )(phys, start, slopes, qb, knb, vnb, *([cache_k] * n_pg), *([cache_v] * n_pg))


def _pad_rows(x, rows):
    return jnp.pad(x, ((0, rows - x.shape[0]),) + ((0, 0),) * (x.ndim - 1))


def _layer_weights(l, g_mix, w_in, b_forget, w_branch_a, w_branch_b, w_out, g_ffn, w_peer_q, peer_sub_keys,
                   peer_u, peer_v):
    d = w_in.shape[1]
    wa_w = w_branch_a.shape[1]
    n_qkv = 6 * wa_w
    w = w_in[l]
    wf = jnp.pad(w[:, n_qkv:n_qkv + N_HEADS], ((0, 0), (0, LANES - N_HEADS)))
    wfh = wf.astype(BF16)
    wfl = (wf - wfh.astype(F32)).astype(BF16)
    sk = peer_sub_keys[l]
    return dict(
        g_mix=g_mix[l].reshape(1, d), wqkv=w[:, :n_qkv].astype(BF16), wfh=wfh, wfl=wfl,
        bf=b_forget[l].reshape(1, N_HEADS), wg=w[:, n_qkv + N_HEADS:].astype(BF16),
        wa=w_branch_a[l].astype(BF16), wb=w_branch_b[l].astype(BF16), wo=w_out[l].astype(BF16),
        g_ffn=g_ffn[l].reshape(1, d), wq=w_peer_q[l].astype(BF16),
        sk=sk.reshape((sk.shape[0] * sk.shape[1],) + sk.shape[2:]).astype(BF16),
        u=peer_u[l].astype(BF16), vt=peer_v[l].T.astype(BF16))


def _ffn(x, oa, ob, ga, gb, wts, g_out, tm, tb, ecn):
    h, hn, st = _merge(x, oa, ob, ga, gb, wts["wa"], wts["wb"], wts["wo"], wts["g_ffn"], wts["wq"], wts["sk"], tm)
    rank2, f2, cnt, c1 = _select(st)
    return _peer(hn, h, rank2, f2, cnt, c1, wts["u"], wts["vt"], g_out, tb, ecn)


def kernel(x_prompt, x_sample, cache_moba_k, cache_moba_v, cache_fox_k, cache_fox_v, cache_fox_logf, page_table,
           g_mix, w_in, b_forget, w_branch_a, w_branch_b, w_out, g_ffn, w_peer_q, peer_sub_keys, peer_u, peer_v,
           g_final):
    batch, seq, d = x_prompt.shape
    nseq, dec_seq, _ = x_sample.shape
    depth = w_in.shape[0]
    assert depth == 1 and dec_seq == 1, "one layer, one new token per sampled sequence"
    npages, page = page_table.shape[1], cache_moba_k.shape[2]
    past = npages * page
    assert past % MOBA_BLOCK == 0 and past // MOBA_BLOCK >= MOBA_TOPK and seq % MOBA_BLOCK == 0
    assert page == LANES
    slopes = 2.0 ** (-8.0 * jnp.arange(1, N_HEADS + 1, dtype=F32) / N_HEADS)
    wts = _layer_weights(0, g_mix, w_in, b_forget, w_branch_a, w_branch_b, w_out, g_ffn, w_peer_q, peer_sub_keys,
                         peer_u, peer_v)
    g_out = g_final.reshape(1, d)
    proj_w = (wts["g_mix"], wts["wqkv"], wts["wfh"], wts["wfl"], wts["bf"], wts["wg"])
    npair = N_HEADS // 2
    ecn = SUBLANES * PEER_NKEYS

    n = batch * seq
    xp = x_prompt.reshape(n, d)
    qa, ka, va, qb, kb, vb, lf, ga, gb = _proj(xp, *proj_w, tm=min(256, n))
    c = _cumsum_rows(lf.reshape(batch, seq, N_HEADS).transpose(1, 0, 2).reshape(seq, batch * N_HEADS))
    c = c.reshape(seq, batch, npair, 2)
    oa = _moba_prompt(slopes, qa, ka, va, batch, seq)
    ob = _fox_prompt(qb, kb, vb, c.transpose(1, 2, 0, 3), batch, seq)
    y_prompt = _ffn(xp, oa, ob, ga, gb, wts, g_out, min(256, n), min(512, n), ecn).reshape(batch, seq, d)
    kv_shape = (1, batch, seq, N_HEADS, HEAD_DIM)
    p_new = (ka.reshape(kv_shape), va.reshape(kv_shape), kb.reshape(kv_shape), vb.reshape(kv_shape),
             lf.reshape(1, batch, seq, N_HEADS))

    ns = -(-nseq // LANES) * LANES
    xs = _pad_rows(x_sample.reshape(nseq, d), ns)
    qa, ka, va, qb, kb, vb, lf, ga, gb = _proj(xs, *proj_w, tm=ns)
    lane_bcast = lambda z: jnp.broadcast_to(z[:nseq].reshape(nseq, N_HEADS, HEAD_DIM, 1),
                                            (nseq, N_HEADS, HEAD_DIM, LANES))
    pages_minor = lambda cache: cache.transpose(0, 1, 3, 4, 2)
    mk, mv = pages_minor(cache_moba_k), pages_minor(cache_moba_v)
    sel = _moba_gate(page_table, lane_bcast(qa), mk)[..., 0]
    ppb = MOBA_BLOCK // page
    sel_pages = sel.transpose(0, 2, 1)[..., None] * ppb + jnp.arange(ppb, dtype=jnp.int32)
    sel_pages = sel_pages.reshape(nseq, N_HEADS * MOBA_TOPK * ppb)
    phys = jnp.take_along_axis(page_table, sel_pages, axis=1).reshape(-1)
    start = (sel_pages * page).reshape(-1)
    oa = _moba_sample(phys, start, slopes, lane_bcast(qa), lane_bcast(ka), lane_bcast(va), mk, mv, past)
    lf_past = _gather_pages(page_table, cache_fox_logf.transpose(0, 1, 3, 2))
    lf_past = lf_past.transpose(0, 1, 3, 2).reshape(nseq, past, N_HEADS)
    lf_all = jnp.concatenate([lf_past, lf[:nseq, None, :]], axis=1)
    tot = -(-(past + 1) // MOBA_BLOCK) * MOBA_BLOCK
    lf_all = jnp.pad(lf_all, ((0, 0), (0, tot - past - 1), (0, 0)))
    cs = _cumsum_rows(lf_all.transpose(1, 0, 2).reshape(tot, nseq * N_HEADS)).reshape(tot, nseq, N_HEADS)
    cq = jnp.broadcast_to(cs[past][:, :, None], (nseq, N_HEADS, LANES))
    ck = cs[:past].reshape(npages, page, nseq, N_HEADS).transpose(2, 0, 3, 1)
    ob = _fox_sample(page_table, lane_bcast(qb), lane_bcast(kb), lane_bcast(vb), cq, ck,
                     pages_minor(cache_fox_k), pages_minor(cache_fox_v))
    oa = _pad_rows(oa.reshape(nseq, N_HEADS * HEAD_DIM), ns)
    ob = _pad_rows(ob.reshape(nseq, N_HEADS * HEAD_DIM), ns)
    y_sample = _ffn(xs, oa, ob, ga, gb, wts, g_out, ns, ns, ecn)[:nseq].reshape(nseq, 1, d)
    kv_shape = (1, nseq, 1, N_HEADS, HEAD_DIM)
    s_new = (ka[:nseq].reshape(kv_shape), va[:nseq].reshape(kv_shape), kb[:nseq].reshape(kv_shape),
             vb[:nseq].reshape(kv_shape), lf[:nseq].reshape(1, nseq, 1, N_HEADS))
    return (y_prompt, y_sample) + p_new + s_new
```

```python
import functools

import jax
import jax.numpy as jnp
from jax import lax
from jax.experimental import pallas as pl
from jax.experimental.pallas import tpu as pltpu

F32 = jnp.float32
BF16 = jnp.bfloat16

HEAD_DIM = 64
N_HEADS = 8
PAIR = 2 * HEAD_DIM
MOBA_BLOCK = 256
MOBA_TOPK = 3
PEER_HEADS = 8
PEER_NKEYS = 128
PEER_TOPK = 16
RMS_EPS = 1e-6
NEG = -1e30
LANES = 128
SUBLANES = 8
BF16_ROWS = 2 * SUBLANES
VMEM_LIMIT = 56 * 1024 * 1024

_NT = (((1,), (1,)), ((), ()))


def _dot(a, b):
    return jnp.dot(a, b, preferred_element_type=F32)


def _dot_nt(a, b, precision=None):
    return lax.dot_general(a, b, _NT, precision=precision, preferred_element_type=F32)


def _params(*sem):
    return pltpu.CompilerParams(dimension_semantics=sem, vmem_limit_bytes=VMEM_LIMIT)


def _half(idx):
    return jnp.where(idx >= HEAD_DIM, 1, 0)


def _rms(x, g):
    return x * lax.rsqrt(jnp.mean(x * x, axis=-1, keepdims=True) + RMS_EPS) * g


def _proj_kernel(x_ref, g_ref, wqkv_ref, wfh_ref, wfl_ref, bf_ref, wg_ref, wkvt_ref,
                 qa_ref, ka_ref, va_ref, qb_ref, kb_ref, vb_ref, lf_ref, ga_ref, gb_ref, *kvt_refs):
    xn = _rms(x_ref[...], g_ref[...])
    xb = xn.astype(BF16)
    width = qa_ref.shape[-1]
    for i, o_ref in enumerate((qa_ref, ka_ref, va_ref, qb_ref, kb_ref, vb_ref)):
        o_ref[...] = _dot(xb, wqkv_ref[:, i * width:(i + 1) * width])
    for i, t_ref in enumerate(kvt_refs):
        t_ref[...] = _dot_nt(wkvt_ref[i * width:(i + 1) * width, :], xb)
    xl = (xn - xb.astype(F32)).astype(BF16)
    zf = _dot(xb, wfh_ref[...]) + (_dot(xl, wfh_ref[...]) + _dot(xb, wfl_ref[...]))
    zf = zf[:, :N_HEADS] + bf_ref[...]
    lf_ref[...] = jnp.minimum(zf, 0.0) - jnp.log1p(jnp.exp(-jnp.abs(zf)))
    d = ga_ref.shape[-1]
    ga_ref[...] = 1.0 / (1.0 + jnp.exp(-_dot(xb, wg_ref[:, :d])))
    gb_ref[...] = 1.0 / (1.0 + jnp.exp(-_dot(xb, wg_ref[:, d:])))


def _proj(x, g, wqkv, wfh, wfl, bfg, wg, wkvt, tm, batch=None):
    n, d = x.shape
    width = wqkv.shape[1] // 6
    row = lambda i: (i, 0)
    fix = lambda i: (0, 0)
    outs = [jax.ShapeDtypeStruct((n, width), F32)] * 6 + [
        jax.ShapeDtypeStruct((n, N_HEADS), F32),
        jax.ShapeDtypeStruct((n, d), F32), jax.ShapeDtypeStruct((n, d), F32)]
    out_specs = [pl.BlockSpec((tm, width), row)] * 6 + [
        pl.BlockSpec((tm, N_HEADS), row), pl.BlockSpec((tm, d), row), pl.BlockSpec((tm, d), row)]
    if batch is not None:
        seq = n // batch
        nq = seq // tm
        outs += [jax.ShapeDtypeStruct((batch, width, seq), F32)] * 4
        out_specs += [pl.BlockSpec((None, width, tm), lambda i: (i // nq, 0, i % nq))] * 4
    return pl.pallas_call(
        _proj_kernel,
        grid=(n // tm,),
        in_specs=[pl.BlockSpec((tm, d), row), pl.BlockSpec((1, d), fix),
                  pl.BlockSpec(wqkv.shape, fix), pl.BlockSpec(wfh.shape, fix),
                  pl.BlockSpec(wfl.shape, fix), pl.BlockSpec((1, N_HEADS), fix),
                  pl.BlockSpec(wg.shape, fix), pl.BlockSpec(wkvt.shape, fix)],
        out_specs=out_specs,
        out_shape=outs,
        compiler_params=_params("parallel"),
        name="proj",
    )(x, g, wqkv, wfh, wfl, bfg, wg, wkvt)


def _cumsum_kernel(x_ref, o_ref, carry_ref):
    @pl.when(pl.program_id(0) == 0)
    def _():
        carry_ref[...] = jnp.zeros_like(carry_ref)

    blk = x_ref.shape[0]
    r = lax.broadcasted_iota(jnp.int32, (blk, blk), 0)
    c = lax.broadcasted_iota(jnp.int32, (blk, blk), 1)
    tri = jnp.where(c <= r, 1.0, 0.0).astype(F32)
    y = jnp.dot(tri, x_ref[...], precision=lax.Precision.HIGHEST,
                preferred_element_type=F32) + carry_ref[...]
    o_ref[...] = y
    carry_ref[...] = y[blk - 1:blk, :]


def _cumsum_rows(x, blk=256):
    n, c = x.shape
    return pl.pallas_call(
        _cumsum_kernel,
        grid=(n // blk,),
        in_specs=[pl.BlockSpec((blk, c), lambda i: (i, 0))],
        out_specs=pl.BlockSpec((blk, c), lambda i: (i, 0)),
        out_shape=jax.ShapeDtypeStruct((n, c), F32),
        scratch_shapes=[pltpu.VMEM((1, c), F32)],
        compiler_params=_params("arbitrary"),
        name="cumsum",
    )(x)


def _stage_kv(k_ref, v_ref, kb_s, vt_s, kmean_s):
    nblk = k_ref.shape[0] // MOBA_BLOCK
    for c in range(nblk):
        rows = slice(c * MOBA_BLOCK, (c + 1) * MOBA_BLOCK)
        kc = k_ref[rows, :]
        if kb_s is not None:
            kb_s[rows, :] = kc.astype(BF16)
        if kmean_s is not None:
            kmean_s[c:c + 1, :] = jnp.sum(kc, axis=0, keepdims=True) * (1.0 / MOBA_BLOCK)
        vt_s[:, rows] = v_ref[rows, :].T.astype(BF16)


BIG = 1e30
LOG2E = 1.4426950408889634
PAST, OWN, DEAD = 0, 1, 2


def _block_rows(blk):
    return pl.ds(pl.multiple_of(blk * MOBA_BLOCK, MOBA_BLOCK), MOBA_BLOCK)


def _flash_sweep(own, qs, kb_fn, vt_s, bias_s, row_fn, qk_s, p_s, acc_s, o_ref):
    nhq = qs[0].shape[0] // LANES
    chains = [(e, hq) for e in range(2) for hq in range(nhq)]
    rows = _block_rows
    block_of = lambda t: jnp.where(t == 0, own, jnp.maximum(jnp.minimum(t, own) - 1, 0))

    def issue_qk(blk, slot):
        for e in range(2):
            raw = _dot_nt(kb_fn(e, blk), qs[e])
            for hq in range(nhq):
                qk_s[slot, e * nhq + hq] = raw[:, _lane_half(hq)]

    def issue_pv(blk, slot, alphas):
        for e in range(2):
            pv = _dot(vt_s[_head_rows(e), rows(blk)], p_s[slot, e])
            for hq in range(nhq):
                c = e * nhq + hq
                acc_s[c] = alphas[c] * acc_s[c] + pv[:, _lane_half(hq)]

    acc_s[...] = jnp.zeros_like(acc_s)
    p_s[1] = jnp.zeros(p_s.shape[1:], p_s.dtype)
    issue_qk(own, 0)
    row = lambda v: jnp.full((1, LANES), v, F32)
    init = (tuple(row(NEG) for _ in chains), tuple(row(0.0) for _ in chains), tuple(row(1.0) for _ in chains))

    def trip(t, slot, carry):
        ms, ls, alphas = carry
        issue_pv(block_of(jnp.maximum(t - 1, 0)), 1 - slot, alphas)
        issue_qk(block_of(t + 1), 1 - slot)
        blk = block_of(t)
        kind = jnp.where(t > own, DEAD, jnp.where(t == 0, OWN, PAST))
        off = (own - blk) * MOBA_BLOCK
        new_m, new_l, new_a = [], [], []
        for c, (e, hq) in enumerate(chains):
            lg = qk_s[slot, c] - bias_s[kind, c]
            if row_fn is not None:
                lg = lg + row_fn(e, hq, blk, off)
            mn = jnp.maximum(ms[c], jnp.max(lg, axis=0, keepdims=True))
            alpha = jnp.exp2(ms[c] - mn)
            p = jnp.exp2(lg - mn)
            p_s[slot, e, :, _lane_half(hq)] = p.astype(BF16)
            new_m.append(mn)
            new_l.append(alpha * ls[c] + jnp.sum(p, axis=0, keepdims=True))
            new_a.append(alpha)
        return tuple(new_m), tuple(new_l), tuple(new_a)

    n_pairs = own // 2 + 1
    ms, ls, alphas = lax.fori_loop(0, n_pairs, lambda g, c: trip(2 * g + 1, 1, trip(2 * g, 0, c)), init)
    issue_pv(block_of(2 * n_pairs - 1), 1, alphas)
    heads = [jnp.concatenate([acc_s[e * nhq + hq] / ls[e * nhq + hq] for hq in range(nhq)], axis=1)
             for e in range(2)]
    o_ref[...] = jnp.concatenate(heads, axis=0).T


def _head_rows(e):
    return slice(e * HEAD_DIM, (e + 1) * HEAD_DIM)


def _lane_half(hq):
    return slice(hq * LANES, (hq + 1) * LANES)


def _sweep_scratch(tq):
    nchain = 2 * (tq // LANES)
    return [pltpu.VMEM((3, nchain, MOBA_BLOCK, LANES), F32),
            pltpu.VMEM((2, nchain, MOBA_BLOCK, LANES), F32), pltpu.VMEM((2, 2, MOBA_BLOCK, tq), BF16),
            pltpu.VMEM((nchain, HEAD_DIM, LANES), F32)]


def _rel_pos(hq):
    return (lax.broadcasted_iota(jnp.int32, (MOBA_BLOCK, LANES), 1) + hq * LANES
            - lax.broadcasted_iota(jnp.int32, (MOBA_BLOCK, LANES), 0))


def _moba_kernel(slopes_ref, q_ref, k_ref, v_ref, o_ref, kb_s, vt_s, kmean_s, sel_s, bias_s, qk_s, p_s, acc_s):
    hp = pl.program_id(1)
    own = pl.program_id(2)
    tq = q_ref.shape[0]
    nblk = kmean_s.shape[0]
    nhq = tq // LANES
    slopes2 = [slopes_ref[hp * 2 + e] * LOG2E for e in range(2)]

    @pl.when(own == 0)
    def _():
        _stage_kv(k_ref, v_ref, kb_s, vt_s, kmean_s)
        for e in range(2):
            for hq in range(nhq):
                rel = _rel_pos(hq)
                alibi = slopes2[e] * rel.astype(F32)
                bias_s[PAST, e * nhq + hq] = alibi
                bias_s[OWN, e * nhq + hq] = jnp.where(rel >= 0, alibi, BIG)
                bias_s[DEAD, e * nhq + hq] = jnp.full(alibi.shape, BIG, F32)

    q = q_ref[...]
    lane_head = _half(lax.broadcasted_iota(jnp.int32, (1, PAIR), 1))
    blk = lax.broadcasted_iota(jnp.int32, (nblk, tq), 0)
    qs = []
    for e in range(2):
        qm = jnp.where(lane_head == e, q, 0.0)
        gate = _dot_nt(kmean_s[...], qm, precision=lax.Precision.HIGHEST)
        valid = blk < own
        gate = jnp.where(valid, gate, -jnp.inf)
        rank = jnp.zeros((nblk, tq), F32)
        for mth in range(nblk):
            g_m = gate[mth:mth + 1, :]
            beats = jnp.where(g_m > gate, 1.0, jnp.where((g_m == gate) & (blk > mth), 1.0, 0.0))
            rank = rank + beats
        sel = jnp.where((valid & (rank < MOBA_TOPK)) | (blk == own), 1.0, 0.0)
        for mth in range(nblk):
            sel_s[e, mth] = jnp.broadcast_to(sel[mth:mth + 1, :], sel_s.shape[2:])
        qe = (qm * (HEAD_DIM ** -0.5 * LOG2E)).astype(BF16)
        qs.append(qe)

    def row_fn(e, hq, b, off):
        picked = sel_s[e, b][0:1, _lane_half(hq)]
        return (picked - 1.0) * BIG - slopes2[e] * off.astype(F32)

    _flash_sweep(own, qs, lambda e, b: kb_s[_block_rows(b), :], vt_s, bias_s, row_fn, qk_s, p_s, acc_s, o_ref)


def _moba_prompt(slopes, q, k, v, batch, seq):
    n, width = q.shape
    tq = MOBA_BLOCK
    nq = seq // tq
    npair = width // PAIR
    return pl.pallas_call(
        _moba_kernel,
        grid=(batch, npair, nq),
        in_specs=[pl.BlockSpec(memory_space=pltpu.SMEM),
                  pl.BlockSpec((tq, PAIR), lambda b, h, i: (b * nq + i, h)),
                  pl.BlockSpec((seq, PAIR), lambda b, h, i: (b, h)),
                  pl.BlockSpec((seq, PAIR), lambda b, h, i: (b, h))],
        out_specs=pl.BlockSpec((tq, PAIR), lambda b, h, i: (b * nq + i, h)),
        out_shape=jax.ShapeDtypeStruct((n, width), F32),
        scratch_shapes=[pltpu.VMEM((seq, PAIR), BF16), pltpu.VMEM((PAIR, seq), BF16),
                        pltpu.VMEM((seq // MOBA_BLOCK, PAIR), F32),
                        pltpu.VMEM((2, seq // MOBA_BLOCK, SUBLANES, tq), F32)] + _sweep_scratch(tq),
        compiler_params=_params("parallel", "parallel", "arbitrary"),
        name="moba",
    )(slopes, q, k, v)


def _split3(x):
    hi = x.astype(BF16).astype(F32)
    mid = (x - hi).astype(BF16).astype(F32)
    lo = ((x - hi) - mid).astype(BF16).astype(F32)
    return hi, mid, lo


def _with_bias_lanes(x, e, ones_at, column, column_at):
    lane = lax.broadcasted_iota(jnp.int32, (1, PAIR), 1)
    base = (1 - e) * HEAD_DIM
    out = jnp.where(_half(lane) == e, x, 0.0)
    out = jnp.where((lane >= base + ones_at) & (lane < base + ones_at + 3), 1.0, out)
    for i, part in enumerate(_split3(column)):
        out = jnp.where(lane == base + column_at + i, part, out)
    return out


def _fox_kernel(q_ref, k_ref, v_ref, c_ref, o_ref, kb_s, vt_s, bias_s, qk_s, p_s, acc_s):
    own = pl.program_id(2)
    tq = q_ref.shape[0]
    nhq = tq // LANES

    @pl.when(own == 0)
    def _():
        _stage_kv(k_ref, v_ref, None, vt_s, None)
        for blk in range(k_ref.shape[0] // MOBA_BLOCK):
            rows = slice(blk * MOBA_BLOCK, (blk + 1) * MOBA_BLOCK)
            for e in range(2):
                kb_s[e, rows, :] = _with_bias_lanes(k_ref[rows, :], e, 3, c_ref[rows, e:e + 1] * -LOG2E, 0).astype(BF16)
        for c in range(2 * nhq):
            bias_s[PAST, c] = jnp.zeros(bias_s.shape[2:], F32)
            bias_s[OWN, c] = jnp.where(_rel_pos(c % nhq) >= 0, 0.0, BIG)
            bias_s[DEAD, c] = jnp.full(bias_s.shape[2:], BIG, F32)

    q = q_ref[...] * (HEAD_DIM ** -0.5 * LOG2E)
    qs = [_with_bias_lanes(q, e, 0, c_ref[_block_rows(own), e:e + 1] * LOG2E, 3).astype(BF16) for e in range(2)]
    _flash_sweep(own, qs, lambda e, b: kb_s[e, _block_rows(b), :], vt_s, bias_s, None, qk_s, p_s, acc_s, o_ref)


def _fox_prompt(q, k, v, c, batch, seq):
    n, width = q.shape
    tq = MOBA_BLOCK
    nq = seq // tq
    npair = width // PAIR
    return pl.pallas_call(
        _fox_kernel,
        grid=(batch, npair, nq),
        in_specs=[pl.BlockSpec((tq, PAIR), lambda b, h, i: (b * nq + i, h)),
                  pl.BlockSpec((seq, PAIR), lambda b, h, i: (b, h)),
                  pl.BlockSpec((seq, PAIR), lambda b, h, i: (b, h)),
                  pl.BlockSpec((None, None, seq, 2), lambda b, h, i: (b, h, 0, 0))],
        out_specs=pl.BlockSpec((tq, PAIR), lambda b, h, i: (b * nq + i, h)),
        out_shape=jax.ShapeDtypeStruct((n, width), F32),
        scratch_shapes=[pltpu.VMEM((2, seq, PAIR), BF16), pltpu.VMEM((PAIR, seq), BF16)] + _sweep_scratch(tq),
        compiler_params=_params("parallel", "parallel", "arbitrary"),
        name="fox",
    )(q, k, v, c)


def _merge_kernel(x_ref, oa_ref, ob_ref, ga_ref, gb_ref, wa_ref, wb_ref, wo_ref, g_ref, wq_ref, sk_ref,
                  h_ref, hn_ref, st_ref):
    a = _dot(oa_ref[...].astype(BF16), wa_ref[...])
    b = _dot(ob_ref[...].astype(BF16), wb_ref[...])
    mix = ga_ref[...] * a + gb_ref[...] * b
    h = x_ref[...] + _dot(mix.astype(BF16), wo_ref[...])
    h_ref[...] = h
    hn = _rms(h, g_ref[...]).astype(BF16)
    hn_ref[...] = hn
    for i in range(0, st_ref.shape[0], 2):
        qp = _dot(hn, wq_ref[:, i * PEER_NKEYS:(i + 2) * PEER_NKEYS]).astype(BF16)
        for j in range(2):
            st_ref[i + j] = _dot_nt(sk_ref[i + j], qp[:, j * PEER_NKEYS:(j + 1) * PEER_NKEYS])


def _merge(x, oa, ob, ga, gb, wa, wb, wo, g, wq, sk, tm):
    n, d = x.shape
    width = oa.shape[1]
    nsk = sk.shape[0]
    row = lambda i: (i, 0)
    fix = lambda i: (0, 0)
    return pl.pallas_call(
        _merge_kernel,
        grid=(n // tm,),
        in_specs=[pl.BlockSpec((tm, d), row), pl.BlockSpec((tm, width), row), pl.BlockSpec((tm, width), row),
                  pl.BlockSpec((tm, d), row), pl.BlockSpec((tm, d), row),
                  pl.BlockSpec(wa.shape, fix), pl.BlockSpec(wb.shape, fix), pl.BlockSpec(wo.shape, fix),
                  pl.BlockSpec((1, d), fix), pl.BlockSpec(wq.shape, fix),
                  pl.BlockSpec(sk.shape, lambda i: (0, 0, 0))],
        out_specs=[pl.BlockSpec((tm, d), row), pl.BlockSpec((tm, d), row),
                   pl.BlockSpec((nsk, PEER_NKEYS, tm), lambda i: (0, 0, i))],
        out_shape=[jax.ShapeDtypeStruct((n, d), F32), jax.ShapeDtypeStruct((n, d), BF16),
                   jax.ShapeDtypeStruct((nsk, PEER_NKEYS, n), F32)],
        compiler_params=_params("parallel"),
        name="merge",
    )(x, oa, ob, ga, gb, wa, wb, wo, g, wq, sk)


def _cand_rows():
    rows = []
    for j in range(PEER_TOPK):
        rows.append(PEER_TOPK // (j + 1))
    return rows


def _extract_sorted(vals, count, dst_ref=None):
    sub = lax.broadcasted_iota(jnp.int32, vals.shape, 0).astype(F32)
    out = []
    for r in range(count):
        mx = jnp.max(vals, axis=0, keepdims=True)
        out.append(mx)
        if dst_ref is not None:
            dst_ref[r:r + 1, :] = mx
        first = jnp.min(jnp.where(vals == mx, sub, float(vals.shape[0])), axis=0, keepdims=True)
        vals = jnp.where(sub == first, -jnp.inf, vals)
    return out


def _bf16_twice(x):
    bits = pltpu.bitcast(x.astype(BF16).astype(F32), jnp.uint32)
    return bits | lax.shift_right_logical(bits, jnp.uint32(16))


def _select_head(h, st_ref, rank2_ref, f2_ref, cnt_ref, c1_ref, a_s, b_s, cand_s):
    rows = _cand_rows()
    s1 = st_ref[2 * h]
    s2 = st_ref[2 * h + 1]
    _extract_sorted(s1, PEER_TOPK, a_s)
    _extract_sorted(s2, PEER_TOPK, b_s)
    a = a_s[...]
    b = b_s[...]
    cand_s[...] = jnp.full(cand_s.shape, -jnp.inf, F32)
    off = 0
    for j, lim in enumerate(rows):
        cand_s[off:off + lim, :] = a[0:lim, :] + b[j:j + 1, :]
        off += lim
    tops = _extract_sorted(cand_s[...], PEER_TOPK)
    theta = tops[PEER_TOPK - 1]
    z = jnp.ones_like(theta)
    for t in tops[1:]:
        z = z + jnp.exp(t - tops[0])
    cnt = jnp.zeros(s1.shape, F32)
    rank2 = jnp.zeros(s2.shape, F32)
    for j in range(PEER_TOPK):
        bj = b[j:j + 1, :]
        cnt = cnt + jnp.where(s1 + bj >= theta, 1.0, 0.0)
        rank2 = rank2 + jnp.where(bj > s2, 1.0, 0.0)
    cnt_ref[h] = _bf16_twice(cnt)
    rank2_ref[h] = pltpu.bitcast(rank2.astype(BF16), jnp.uint32)
    f2_ref[h] = pltpu.bitcast(jnp.exp(s2 - b[0:1, :]).astype(BF16), jnp.uint32)
    c1_ref[h] = _bf16_twice(jnp.exp(s1 - a[0:1, :]) / z)


def _select_kernel(st_ref, rank2_ref, f2_ref, cnt_ref, c1_ref, *scratch):
    def pair(hp, carry):
        for e in range(2):
            _select_head(2 * hp + e, st_ref, rank2_ref, f2_ref, cnt_ref, c1_ref, *scratch[3 * e:3 * e + 3])
        return carry

    lax.fori_loop(0, PEER_HEADS // 2, pair, 0)


def _select(st, tl=LANES):
    nsk, nk, n = st.shape
    spec_in = pl.BlockSpec((nsk, nk, tl), lambda i: (0, 0, i))
    spec_out = lambda rows: pl.BlockSpec((PEER_HEADS, rows, tl), lambda i: (0, 0, i))
    ncand = -(-sum(_cand_rows()) // SUBLANES) * SUBLANES
    scratch = [pltpu.VMEM((PEER_TOPK, tl), F32), pltpu.VMEM((PEER_TOPK, tl), F32), pltpu.VMEM((ncand, tl), F32)]
    shape = lambda rows: jax.ShapeDtypeStruct((PEER_HEADS, rows, n), jnp.uint32)
    return pl.pallas_call(
        _select_kernel,
        grid=(n // tl,),
        in_specs=[spec_in],
        out_specs=[spec_out(nk // 2), spec_out(nk // 2), spec_out(nk), spec_out(nk)],
        out_shape=[shape(nk // 2), shape(nk // 2), shape(nk), shape(nk)],
        scratch_shapes=scratch * 2,
        compiler_params=_params("parallel"),
        name="select",
    )(st)


def _gelu(x):
    return 0.5 * x * (1.0 + jnp.tanh(0.7978845608028654 * (x + 0.044715 * (x * x * x))))


def _peer_weights(il, tc, g0, rank2_ref, f2_ref, cnt_ref, c1_ref):
    cols = slice(tc * LANES, (tc + 1) * LANES)
    cnt, c1 = [], []
    for h in range(PEER_HEADS):
        grp = pl.ds(pl.multiple_of(h * PEER_NKEYS + g0, SUBLANES), SUBLANES)
        for ref, dst in ((cnt_ref, cnt), (c1_ref, c1)):
            row = jnp.broadcast_to(ref[grp, cols][il:il + 1, :], (SUBLANES, LANES))
            dst.append(pltpu.bitcast(row, BF16))
    slabs = []
    for r in range(PEER_NKEYS // BF16_ROWS):
        w = jnp.zeros((BF16_ROWS, LANES), BF16)
        for h in range(PEER_HEADS):
            words = slice((h * PEER_NKEYS + r * BF16_ROWS) // 2, (h * PEER_NKEYS + (r + 1) * BF16_ROWS) // 2)
            rank2 = pltpu.bitcast(rank2_ref[words, cols], BF16)
            f2 = pltpu.bitcast(f2_ref[words, cols], BF16)
            w = w + jnp.where(rank2 < cnt[h], f2 * c1[h], 0.0)
        slabs.append(w)
    return slabs


def _peer_kernel(hn_ref, h_ref, rank2_ref, f2_ref, cnt_ref, c1_ref, u_ref, vtp_ref, vtc_ref, g_ref, y_ref,
                 acc_s, s_s, pa_s, pb_s):
    ec = pl.program_id(1)
    last = pl.num_programs(1) - 1
    ecn, tb = pa_s.shape
    n_i1 = ecn // PEER_NKEYS
    n_tc = tb // LANES
    assert n_i1 == SUBLANES
    n_half = 2 if n_tc % 2 == 0 else 1
    halves = [slice(k * (tb // n_half), (k + 1) * (tb // n_half)) for k in range(n_half)]

    @pl.when(ec == 0)
    def _():
        acc_s[...] = jnp.zeros_like(acc_s)
        pb_s[...] = jnp.zeros_like(pb_s)

    def step(p_prev, p_cur):
        g0 = pl.multiple_of(ec * n_i1, SUBLANES)
        for cols in halves:
            s_s[:, cols] = _dot_nt(u_ref[...], hn_ref[cols, :])
        acc_s[...] += _dot(vtp_ref[...], p_prev[...])
        for tc in range(n_tc):
            cols = slice(tc * LANES, (tc + 1) * LANES)
            for il in range(n_i1):
                for r, w in enumerate(_peer_weights(il, tc, g0, rank2_ref, f2_ref, cnt_ref, c1_ref)):
                    rows = slice(il * PEER_NKEYS + r * BF16_ROWS, il * PEER_NKEYS + (r + 1) * BF16_ROWS)
                    p_cur[rows, cols] = w * _gelu(s_s[rows, cols].astype(BF16))

        @pl.when(ec == last)
        def _():
            y_ref[...] = _rms(h_ref[...] + (acc_s[...] + _dot(vtc_ref[...], p_cur[...])).T, g_ref[...])

    pl.when(ec % 2 == 0)(lambda: step(pb_s, pa_s))
    pl.when(ec % 2 == 1)(lambda: step(pa_s, pb_s))


def _peer(hn, h, rank2, f2, cnt, c1, u, vt, g, tb, ecn):
    n, d = h.shape
    nchunk = u.shape[0] // ecn
    assert nchunk >= 2
    tok = lambda t, e: (t, 0)
    tables = [z.reshape(-1, n) for z in (rank2, f2, cnt, c1)]
    sel = [pl.BlockSpec((z.shape[0], tb), lambda t, e: (0, t)) for z in tables]
    rank2, f2, cnt, c1 = tables
    return pl.pallas_call(
        _peer_kernel,
        grid=(n // tb, nchunk),
        in_specs=[pl.BlockSpec((tb, d), tok), pl.BlockSpec((tb, d), tok)] + sel + [
                  pl.BlockSpec((ecn, d), lambda t, e: (e, 0)),
                  pl.BlockSpec((d, ecn), lambda t, e: (0, jnp.maximum(e - 1, 0))),
                  pl.BlockSpec((d, ecn), lambda t, e: (0, (e // (nchunk - 1)) * (nchunk - 1))),
                  pl.BlockSpec((1, d), lambda t, e: (0, 0))],
        out_specs=pl.BlockSpec((tb, d), tok),
        out_shape=jax.ShapeDtypeStruct((n, d), F32),
        scratch_shapes=[pltpu.VMEM((d, tb), F32), pltpu.VMEM((ecn, tb), F32),
                        pltpu.VMEM((ecn, tb), BF16), pltpu.VMEM((ecn, tb), BF16)],
        compiler_params=_params("parallel", "arbitrary"),
        name="peer",
    )(hn, h, rank2, f2, cnt, c1, u, vt, vt, g)


def _page_index(per_step, i):
    def index(s, p, pt):
        return (0, pt[s, p * per_step + i], 0, 0, 0)
    return index


def _gather_pages_kernel(pt_ref, *refs):
    o_ref = refs[-1]
    for i in range(len(refs) - 1):
        o_ref[i] = refs[i][...]


def _gather_pages(page_table, cache_t, per_step=32):
    nseq, npages = page_table.shape
    heads, page = cache_t.shape[2:]
    per_step = min(per_step, npages)
    grid_spec = pltpu.PrefetchScalarGridSpec(
        num_scalar_prefetch=1,
        grid=(nseq, npages // per_step),
        in_specs=[pl.BlockSpec((None, None, heads, page),
                               (lambda i: lambda s, p, pt: (0, pt[s, p * per_step + i], 0, 0))(i))
                  for i in range(per_step)],
        out_specs=pl.BlockSpec((None, per_step, heads, page), lambda s, p, pt: (s, p, 0, 0)),
    )
    return pl.pallas_call(
        _gather_pages_kernel, grid_spec=grid_spec,
        out_shape=jax.ShapeDtypeStruct((nseq, npages, heads, page), F32),
        compiler_params=_params("parallel", "arbitrary"),
        name="gather_logf",
    )(page_table, *([cache_t] * per_step))


def _fox_sample_kernel(pt_ref, q_ref, kn_ref, vn_ref, cq_ref, ck_ref, *refs):
    n_pg = (len(refs) - 4) // 2
    k_refs = refs[:n_pg]
    v_refs = refs[n_pg:2 * n_pg]
    o_ref, m_s, l_s, acc_s = refs[2 * n_pg:]
    step = pl.program_id(1)
    heads = q_ref.shape[0]

    @pl.when(step == 0)
    def _():
        m_s[...] = jnp.full(m_s.shape, NEG, F32)
        l_s[...] = jnp.zeros_like(l_s)
        acc_s[...] = jnp.zeros_like(acc_s)

    for h in range(heads):
        q = q_ref[h] * HEAD_DIM ** -0.5
        bias_q = cq_ref[h:h + 1, :]
        lgs = [jnp.sum(q * k_refs[i][h], axis=0, keepdims=True) + (bias_q - ck_ref[i, h:h + 1, :])
               for i in range(n_pg)]
        top = lgs[0]
        for lg in lgs[1:]:
            top = jnp.maximum(top, lg)
        m = m_s[h:h + 1, :]
        mn = jnp.maximum(m, jnp.max(top, axis=1, keepdims=True))
        alpha = jnp.exp(m - mn)
        l = alpha * l_s[h:h + 1, :]
        acc = alpha * acc_s[h]
        for i in range(n_pg):
            p = jnp.exp(lgs[i] - mn)
            l = l + p
            acc = acc + p * v_refs[i][h]
        m_s[h:h + 1, :] = mn
        l_s[h:h + 1, :] = l
        acc_s[h] = acc

    @pl.when(step == pl.num_programs(1) - 1)
    def _():
        for h in range(heads):
            q = q_ref[h] * HEAD_DIM ** -0.5
            lg_new = jnp.sum(q * kn_ref[h], axis=0, keepdims=True)
            m = m_s[h:h + 1, :]
            mf = jnp.maximum(m, lg_new)
            a = jnp.exp(m - mf)[:, 0:1]
            b = jnp.exp(lg_new - mf)[:, 0:1]
            num = a * jnp.sum(acc_s[h], axis=1, keepdims=True) + b * vn_ref[h][:, 0:1]
            den = a * jnp.sum(l_s[h:h + 1, :], axis=1, keepdims=True) + b
            o_ref[h] = num / den


def _fox_sample(page_table, qb, knb, vnb, cq, ck, cache_k, cache_v, per_step=16):
    nseq, npages = page_table.shape
    heads, hd, page = cache_k.shape[2:]
    per_step = min(per_step, npages)
    tok = pl.BlockSpec((None, heads, hd, LANES), lambda s, p, pt: (s, 0, 0, 0))
    pages = [pl.BlockSpec((None, None, heads, hd, page), _page_index(per_step, i)) for i in range(per_step)]
    grid_spec = pltpu.PrefetchScalarGridSpec(
        num_scalar_prefetch=1,
        grid=(nseq, npages // per_step),
        in_specs=[tok, tok, tok,
                  pl.BlockSpec((None, heads, LANES), lambda s, p, pt: (s, 0, 0)),
                  pl.BlockSpec((None, per_step, heads, page), lambda s, p, pt: (s, p, 0, 0))] + pages + pages,
        out_specs=pl.BlockSpec((None, heads, hd, 1), lambda s, p, pt: (s, 0, 0, 0)),
        scratch_shapes=[pltpu.VMEM((heads, LANES), F32), pltpu.VMEM((heads, LANES), F32),
                        pltpu.VMEM((heads, hd, LANES), F32)],
    )
    return pl.pallas_call(
        _fox_sample_kernel, grid_spec=grid_spec,
        out_shape=jax.ShapeDtypeStruct((nseq, heads, hd, 1), F32),
        compiler_params=_params("parallel", "arbitrary"),
        name="fox_sample",
    )(page_table, qb, knb, vnb, cq, ck, *([cache_k] * per_step), *([cache_v] * per_step))


def _moba_gate_kernel(pt_ref, q_ref, *refs):
    n_pg = len(refs) - 2
    k_refs = refs[:n_pg]
    sel_ref, gate_s = refs[n_pg:]
    step = pl.program_id(1)
    heads, _, page = k_refs[0].shape
    pages_per_block = MOBA_BLOCK // page
    blocks_per_step = n_pg // pages_per_block
    for h in range(heads):
        q = q_ref[h]
        for i in range(blocks_per_step):
            t = jnp.sum(q * k_refs[i * pages_per_block][h], axis=0, keepdims=True)
            for j in range(1, pages_per_block):
                t = t + jnp.sum(q * k_refs[i * pages_per_block + j][h], axis=0, keepdims=True)
            gate = jnp.sum(t, axis=1, keepdims=True) * (1.0 / MOBA_BLOCK)
            gate_s[step * blocks_per_step + i, h:h + 1, :] = jnp.broadcast_to(gate, (1, LANES))

    @pl.when(step == pl.num_programs(1) - 1)
    def _():
        g = gate_s[...]
        blk = lax.broadcasted_iota(jnp.int32, g.shape, 0)
        for r in range(MOBA_TOPK):
            mx = jnp.max(g, axis=0, keepdims=True)
            first = jnp.min(jnp.where(g == mx, blk, g.shape[0]), axis=0, keepdims=True)
            sel_ref[r] = first[0]
            g = jnp.where(blk == first, -jnp.inf, g)


def _moba_gate(page_table, qb, cache_k, per_step=16):
    nseq, npages = page_table.shape
    heads, hd, page = cache_k.shape[2:]
    per_step = min(per_step, npages)
    nblk = npages * page // MOBA_BLOCK
    grid_spec = pltpu.PrefetchScalarGridSpec(
        num_scalar_prefetch=1,
        grid=(nseq, npages // per_step),
        in_specs=[pl.BlockSpec((None, heads, hd, LANES), lambda s, p, pt: (s, 0, 0, 0))]
        + [pl.BlockSpec((None, None, heads, hd, page), _page_index(per_step, i)) for i in range(per_step)],
        out_specs=pl.BlockSpec((None, MOBA_TOPK, heads, LANES), lambda s, p, pt: (s, 0, 0, 0)),
        scratch_shapes=[pltpu.VMEM((nblk, heads, LANES), F32)],
    )
    return pl.pallas_call(
        _moba_gate_kernel, grid_spec=grid_spec,
        out_shape=jax.ShapeDtypeStruct((nseq, MOBA_TOPK, heads, LANES), jnp.int32),
        compiler_params=_params("parallel", "arbitrary"),
        name="moba_gate",
    )(page_table, qb, *([cache_k] * per_step))


def _moba_sample_kernel(phys_ref, start_ref, slopes_ref, q_ref, kn_ref, vn_ref, *refs, past):
    n_pg = (len(refs) - 1) // 2
    k_refs = refs[:n_pg]
    v_refs = refs[n_pg:2 * n_pg]
    o_ref = refs[-1]
    s_id, h = pl.program_id(0), pl.program_id(1)
    heads = pl.num_programs(1)
    q = q_ref[...] * HEAD_DIM ** -0.5
    slope = slopes_ref[h]
    lane = lax.broadcasted_iota(jnp.int32, (1, k_refs[0].shape[-1]), 1)
    lg_new = jnp.sum(q * kn_ref[...], axis=0, keepdims=True)
    lgs = []
    top = None
    for i in range(n_pg):
        start = start_ref[(s_id * heads + h) * n_pg + i]
        dist = (past - start - lane).astype(F32)
        lg = jnp.sum(q * k_refs[i][...], axis=0, keepdims=True) - slope * dist
        lgs.append(lg)
        top = lg if top is None else jnp.maximum(top, lg)
    m = jnp.maximum(lg_new, jnp.max(top, axis=1, keepdims=True))
    b = jnp.exp(lg_new - m)
    l = jnp.zeros_like(top)
    acc = jnp.zeros(q.shape, F32)
    for i in range(n_pg):
        p = jnp.exp(lgs[i] - m)
        l = l + p
        acc = acc + p * v_refs[i][...]
    num = jnp.sum(acc, axis=1, keepdims=True) + b[:, 0:1] * vn_ref[...][:, 0:1]
    den = jnp.sum(l, axis=1, keepdims=True) + b[:, 0:1]
    o_ref[...] = num / den


def _moba_sample(phys, start, slopes, qb, knb, vnb, cache_k, cache_v, past):
    nseq, heads, hd, _ = qb.shape
    page = cache_k.shape[-1]
    n_pg = phys.shape[0] // (nseq * heads)
    tok = pl.BlockSpec((None, None, hd, LANES), lambda s, h, ph, st: (s, h, 0, 0))

    def pg(i):
        return pl.BlockSpec((None, None, None, hd, page),
                            lambda s, h, ph, st: (0, ph[(s * heads + h) * n_pg + i], h, 0, 0))

    pages = [pg(i) for i in range(n_pg)]
    grid_spec = pltpu.PrefetchScalarGridSpec(
        num_scalar_prefetch=2,
        grid=(nseq, heads),
        in_specs=[pl.BlockSpec(memory_space=pltpu.SMEM), tok, tok, tok] + pages + pages,
        out_specs=pl.BlockSpec((None, None, hd, 1), lambda s, h, ph, st: (s, h, 0, 0)),
    )
    return pl.pallas_call(
        functools.partial(_moba_sample_kernel, past=past), grid_spec=grid_spec,
        out_shape=jax.ShapeDtypeStruct((nseq, heads, hd, 1), F32),
        compiler_params=_params("parallel", "arbitrary"),
        name="moba_sample",
    )(phys, start, slopes, qb, knb, vnb, *([cache_k] * n_pg), *([cache_v] * n_pg))


def _pad_rows(x, rows):
    return jnp.pad(x, ((0, rows - x.shape[0]),) + ((0, 0),) * (x.ndim - 1))


def _layer_weights(l, g_mix, w_in, b_forget, w_branch_a, w_branch_b, w_out, g_ffn, w_peer_q, peer_sub_keys,
                   peer_u, peer_v):
    d = w_in.shape[1]
    wa_w = w_branch_a.shape[1]
    n_qkv = 6 * wa_w
    w = w_in[l]
    wf = jnp.pad(w[:, n_qkv:n_qkv + N_HEADS], ((0, 0), (0, LANES - N_HEADS)))
    wfh = wf.astype(BF16)
    wfl = (wf - wfh.astype(F32)).astype(BF16)
    sk = peer_sub_keys[l]
    wkvt = jnp.concatenate([w[:, i * wa_w:(i + 1) * wa_w] for i in (1, 2, 4, 5)], axis=1).T.astype(BF16)
    return dict(
        g_mix=g_mix[l].reshape(1, d), wqkv=w[:, :n_qkv].astype(BF16), wfh=wfh, wfl=wfl, wkvt=wkvt,
        bf=b_forget[l].reshape(1, N_HEADS), wg=w[:, n_qkv + N_HEADS:].astype(BF16),
        wa=w_branch_a[l].astype(BF16), wb=w_branch_b[l].astype(BF16), wo=w_out[l].astype(BF16),
        g_ffn=g_ffn[l].reshape(1, d), wq=w_peer_q[l].astype(BF16),
        sk=sk.reshape((sk.shape[0] * sk.shape[1],) + sk.shape[2:]).astype(BF16),
        u=peer_u[l].astype(BF16), vt=peer_v[l].T.astype(BF16))


def _ffn(x, oa, ob, ga, gb, wts, g_out, tm, tb, ecn):
    h, hn, st = _merge(x, oa, ob, ga, gb, wts["wa"], wts["wb"], wts["wo"], wts["g_ffn"], wts["wq"], wts["sk"], tm)
    rank2, f2, cnt, c1 = _select(st)
    return _peer(hn, h, rank2, f2, cnt, c1, wts["u"], wts["vt"], g_out, tb, ecn)


def kernel(x_prompt, x_sample, cache_moba_k, cache_moba_v, cache_fox_k, cache_fox_v, cache_fox_logf, page_table,
           g_mix, w_in, b_forget, w_branch_a, w_branch_b, w_out, g_ffn, w_peer_q, peer_sub_keys, peer_u, peer_v,
           g_final):
    batch, seq, d = x_prompt.shape
    nseq, dec_seq, _ = x_sample.shape
    depth = w_in.shape[0]
    assert depth == 1 and dec_seq == 1, "one layer, one new token per sampled sequence"
    npages, page = page_table.shape[1], cache_moba_k.shape[2]
    past = npages * page
    assert past % MOBA_BLOCK == 0 and past // MOBA_BLOCK >= MOBA_TOPK and seq % MOBA_BLOCK == 0
    assert page == LANES
    slopes = 2.0 ** (-8.0 * jnp.arange(1, N_HEADS + 1, dtype=F32) / N_HEADS)
    wts = _layer_weights(0, g_mix, w_in, b_forget, w_branch_a, w_branch_b, w_out, g_ffn, w_peer_q, peer_sub_keys,
                         peer_u, peer_v)
    g_out = g_final.reshape(1, d)
    proj_w = (wts["g_mix"], wts["wqkv"], wts["wfh"], wts["wfl"], wts["bf"], wts["wg"], wts["wkvt"])
    npair = N_HEADS // 2
    ecn = SUBLANES * PEER_NKEYS

    n = batch * seq
    xp = x_prompt.reshape(n, d)
    qa, ka, va, qb, kb, vb, lf, ga, gb, *kv_t = _proj(xp, *proj_w, tm=min(256, seq), batch=batch)
    c = _cumsum_rows(lf.reshape(batch, seq, N_HEADS).transpose(1, 0, 2).reshape(seq, batch * N_HEADS))
    c = c.reshape(seq, batch, npair, 2)
    oa = _moba_prompt(slopes, qa, ka, va, batch, seq)
    ob = _fox_prompt(qb, kb, vb, c.transpose(1, 2, 0, 3), batch, seq)
    y_prompt = _ffn(xp, oa, ob, ga, gb, wts, g_out, min(256, n), min(512, n), ecn).reshape(batch, seq, d)
    p_new = tuple(z.reshape(1, batch, N_HEADS, HEAD_DIM, seq).transpose(0, 1, 4, 2, 3) for z in kv_t) + (
        lf.reshape(1, batch, seq, N_HEADS),)

    ns = -(-nseq // LANES) * LANES
    xs = _pad_rows(x_sample.reshape(nseq, d), ns)
    qa, ka, va, qb, kb, vb, lf, ga, gb = _proj(xs, *proj_w, tm=ns)
    lane_bcast = lambda z: jnp.broadcast_to(z[:nseq].reshape(nseq, N_HEADS, HEAD_DIM, 1),
                                            (nseq, N_HEADS, HEAD_DIM, LANES))
    pages_minor = lambda cache: cache.transpose(0, 1, 3, 4, 2)
    mk, mv = pages_minor(cache_moba_k), pages_minor(cache_moba_v)
    sel = _moba_gate(page_table, lane_bcast(qa), mk)[..., 0]
    ppb = MOBA_BLOCK // page
    sel_pages = sel.transpose(0, 2, 1)[..., None] * ppb + jnp.arange(ppb, dtype=jnp.int32)
    sel_pages = sel_pages.reshape(nseq, N_HEADS * MOBA_TOPK * ppb)
    phys = jnp.take_along_axis(page_table, sel_pages, axis=1).reshape(-1)
    start = (sel_pages * page).reshape(-1)
    oa = _moba_sample(phys, start, slopes, lane_bcast(qa), lane_bcast(ka), lane_bcast(va), mk, mv, past)
    lf_past = _gather_pages(page_table, cache_fox_logf.transpose(0, 1, 3, 2))
    lf_past = lf_past.transpose(0, 1, 3, 2).reshape(nseq, past, N_HEADS)
    lf_all = jnp.concatenate([lf_past, lf[:nseq, None, :]], axis=1)
    tot = -(-(past + 1) // MOBA_BLOCK) * MOBA_BLOCK
    lf_all = jnp.pad(lf_all, ((0, 0), (0, tot - past - 1), (0, 0)))
    cs = _cumsum_rows(lf_all.transpose(1, 0, 2).reshape(tot, nseq * N_HEADS)).reshape(tot, nseq, N_HEADS)
    cq = jnp.broadcast_to(cs[past][:, :, None], (nseq, N_HEADS, LANES))
    ck = cs[:past].reshape(npages, page, nseq, N_HEADS).transpose(2, 0, 3, 1)
    ob = _fox_sample(page_table, lane_bcast(qb), lane_bcast(kb), lane_bcast(vb), cq, ck,
                     pages_minor(cache_fox_k), pages_minor(cache_fox_v))
    oa = _pad_rows(oa.reshape(nseq, N_HEADS * HEAD_DIM), ns)
    ob = _pad_rows(ob.reshape(nseq, N_HEADS * HEAD_DIM), ns)
    y_sample = _ffn(xs, oa, ob, ga, gb, wts, g_out, ns, ns, ecn)[:nseq].reshape(nseq, 1, d)
    kv_shape = (1, nseq, 1, N_HEADS, HEAD_DIM)
    s_new = (ka[:nseq].reshape(kv_shape), va[:nseq].reshape(kv_shape), kb[:nseq].reshape(kv_shape),
             vb[:nseq].reshape(kv_shape), lf[:nseq].reshape(1, nseq, 1, N_HEADS))
    return (y_prompt, y_sample) + p_new + s_new
```

```python
import functools

import jax
import jax.numpy as jnp
from jax import lax
from jax.experimental import pallas as pl
from jax.experimental.pallas import tpu as pltpu

F32 = jnp.float32
BF16 = jnp.bfloat16

HEAD_DIM = 64
N_HEADS = 8
PAIR = 2 * HEAD_DIM
MOBA_BLOCK = 256
MOBA_TOPK = 3
PEER_HEADS = 8
PEER_NKEYS = 128
PEER_TOPK = 16
RMS_EPS = 1e-6
NEG = -1e30
LANES = 128
SUBLANES = 8
BF16_ROWS = 2 * SUBLANES
VMEM_LIMIT = 56 * 1024 * 1024

_NT = (((1,), (1,)), ((), ()))


def _dot(a, b):
    return jnp.dot(a, b, preferred_element_type=F32)


def _dot_nt(a, b, precision=None):
    return lax.dot_general(a, b, _NT, precision=precision, preferred_element_type=F32)


def _params(*sem):
    return pltpu.CompilerParams(dimension_semantics=sem, vmem_limit_bytes=VMEM_LIMIT)


def _half(idx):
    return jnp.where(idx >= HEAD_DIM, 1, 0)


def _rms(x, g):
    return x * lax.rsqrt(jnp.mean(x * x, axis=-1, keepdims=True) + RMS_EPS) * g


def _proj_kernel(x_ref, g_ref, wqkv_ref, wfh_ref, wfl_ref, bf_ref, wg_ref, wkvt_ref,
                 qa_ref, ka_ref, va_ref, qb_ref, kb_ref, vb_ref, lf_ref, ga_ref, gb_ref, *kvt_refs):
    xn = _rms(x_ref[...], g_ref[...])
    xb = xn.astype(BF16)
    width = qa_ref.shape[-1]
    for i, o_ref in enumerate((qa_ref, ka_ref, va_ref, qb_ref, kb_ref, vb_ref)):
        o_ref[...] = _dot(xb, wqkv_ref[:, i * width:(i + 1) * width])
    for i, t_ref in enumerate(kvt_refs):
        t_ref[...] = _dot_nt(wkvt_ref[i * width:(i + 1) * width, :], xb)
    xl = (xn - xb.astype(F32)).astype(BF16)
    zf = _dot(xb, wfh_ref[...]) + (_dot(xl, wfh_ref[...]) + _dot(xb, wfl_ref[...]))
    zf = zf[:, :N_HEADS] + bf_ref[...]
    lf_ref[...] = jnp.minimum(zf, 0.0) - jnp.log1p(jnp.exp(-jnp.abs(zf)))
    d = ga_ref.shape[-1]
    ga_ref[...] = 1.0 / (1.0 + jnp.exp(-_dot(xb, wg_ref[:, :d])))
    gb_ref[...] = 1.0 / (1.0 + jnp.exp(-_dot(xb, wg_ref[:, d:])))


def _proj(x, g, wqkv, wfh, wfl, bfg, wg, wkvt, tm, batch=None):
    n, d = x.shape
    width = wqkv.shape[1] // 6
    row = lambda i: (i, 0)
    fix = lambda i: (0, 0)
    outs = [jax.ShapeDtypeStruct((n, width), F32)] * 6 + [
        jax.ShapeDtypeStruct((n, N_HEADS), F32),
        jax.ShapeDtypeStruct((n, d), F32), jax.ShapeDtypeStruct((n, d), F32)]
    out_specs = [pl.BlockSpec((tm, width), row)] * 6 + [
        pl.BlockSpec((tm, N_HEADS), row), pl.BlockSpec((tm, d), row), pl.BlockSpec((tm, d), row)]
    if batch is not None:
        seq = n // batch
        nq = seq // tm
        outs += [jax.ShapeDtypeStruct((batch, width, seq), F32)] * 4
        out_specs += [pl.BlockSpec((None, width, tm), lambda i: (i // nq, 0, i % nq))] * 4
    return pl.pallas_call(
        _proj_kernel,
        grid=(n // tm,),
        in_specs=[pl.BlockSpec((tm, d), row), pl.BlockSpec((1, d), fix),
                  pl.BlockSpec(wqkv.shape, fix), pl.BlockSpec(wfh.shape, fix),
                  pl.BlockSpec(wfl.shape, fix), pl.BlockSpec((1, N_HEADS), fix),
                  pl.BlockSpec(wg.shape, fix), pl.BlockSpec(wkvt.shape, fix)],
        out_specs=out_specs,
        out_shape=outs,
        compiler_params=_params("parallel"),
        name="proj",
    )(x, g, wqkv, wfh, wfl, bfg, wg, wkvt)


def _cumsum_kernel(x_ref, o_ref, carry_ref):
    @pl.when(pl.program_id(0) == 0)
    def _():
        carry_ref[...] = jnp.zeros_like(carry_ref)

    blk = x_ref.shape[0]
    r = lax.broadcasted_iota(jnp.int32, (blk, blk), 0)
    c = lax.broadcasted_iota(jnp.int32, (blk, blk), 1)
    tri = jnp.where(c <= r, 1.0, 0.0).astype(F32)
    y = jnp.dot(tri, x_ref[...], precision=lax.Precision.HIGHEST,
                preferred_element_type=F32) + carry_ref[...]
    o_ref[...] = y
    carry_ref[...] = y[blk - 1:blk, :]


def _cumsum_rows(x, blk=256):
    n, c = x.shape
    return pl.pallas_call(
        _cumsum_kernel,
        grid=(n // blk,),
        in_specs=[pl.BlockSpec((blk, c), lambda i: (i, 0))],
        out_specs=pl.BlockSpec((blk, c), lambda i: (i, 0)),
        out_shape=jax.ShapeDtypeStruct((n, c), F32),
        scratch_shapes=[pltpu.VMEM((1, c), F32)],
        compiler_params=_params("arbitrary"),
        name="cumsum",
    )(x)


def _stage_kv(k_ref, v_ref, kb_s, vt_s, kmean_s):
    nblk = k_ref.shape[0] // MOBA_BLOCK
    for c in range(nblk):
        rows = slice(c * MOBA_BLOCK, (c + 1) * MOBA_BLOCK)
        kc = k_ref[rows, :]
        if kb_s is not None:
            kb_s[rows, :] = kc.astype(BF16)
        if kmean_s is not None:
            kmean_s[c:c + 1, :] = jnp.sum(kc, axis=0, keepdims=True) * (1.0 / MOBA_BLOCK)
        vt_s[:, rows] = v_ref[rows, :].T.astype(BF16)


BIG = 1e30
LOG2E = 1.4426950408889634
PAST, OWN, DEAD = 0, 1, 2


def _block_rows(blk):
    return pl.ds(pl.multiple_of(blk * MOBA_BLOCK, MOBA_BLOCK), MOBA_BLOCK)


def _flash_sweep(own, qs, kb_fn, vt_s, bias_s, row_fn, qk_s, p_s, acc_s, o_ref):
    nhq = qs[0].shape[0] // LANES
    chains = [(e, hq) for e in range(2) for hq in range(nhq)]
    rows = _block_rows
    block_of = lambda t: jnp.where(t == 0, own, jnp.maximum(jnp.minimum(t, own) - 1, 0))

    def issue_qk(blk, slot):
        for e in range(2):
            raw = _dot_nt(kb_fn(e, blk), qs[e])
            for hq in range(nhq):
                qk_s[slot, e * nhq + hq] = raw[:, _lane_half(hq)]

    def issue_pv(blk, slot, alphas):
        for e in range(2):
            pv = _dot(vt_s[_head_rows(e), rows(blk)], p_s[slot, e])
            for hq in range(nhq):
                c = e * nhq + hq
                acc_s[c] = alphas[c] * acc_s[c] + pv[:, _lane_half(hq)]

    acc_s[...] = jnp.zeros_like(acc_s)
    p_s[1] = jnp.zeros(p_s.shape[1:], p_s.dtype)
    issue_qk(own, 0)
    row = lambda v: jnp.full((1, LANES), v, F32)
    init = (tuple(row(NEG) for _ in chains), tuple(row(0.0) for _ in chains), tuple(row(1.0) for _ in chains))

    def trip(t, slot, carry):
        ms, ls, alphas = carry
        issue_pv(block_of(jnp.maximum(t - 1, 0)), 1 - slot, alphas)
        issue_qk(block_of(t + 1), 1 - slot)
        blk = block_of(t)
        kind = jnp.where(t > own, DEAD, jnp.where(t == 0, OWN, PAST))
        off = (own - blk) * MOBA_BLOCK
        new_m, new_l, new_a = [], [], []
        for c, (e, hq) in enumerate(chains):
            lg = qk_s[slot, c] - bias_s[kind, c]
            if row_fn is not None:
                lg = lg + row_fn(e, hq, blk, off)
            mn = jnp.maximum(ms[c], jnp.max(lg, axis=0, keepdims=True))
            alpha = jnp.exp2(ms[c] - mn)
            p = jnp.exp2(lg - mn)
            p_s[slot, e, :, _lane_half(hq)] = p.astype(BF16)
            new_m.append(mn)
            new_l.append(alpha * ls[c] + jnp.sum(p, axis=0, keepdims=True))
            new_a.append(alpha)
        return tuple(new_m), tuple(new_l), tuple(new_a)

    n_pairs = own // 2 + 1
    ms, ls, alphas = lax.fori_loop(0, n_pairs, lambda g, c: trip(2 * g + 1, 1, trip(2 * g, 0, c)), init)
    issue_pv(block_of(2 * n_pairs - 1), 1, alphas)
    heads = [jnp.concatenate([acc_s[e * nhq + hq] / ls[e * nhq + hq] for hq in range(nhq)], axis=1)
             for e in range(2)]
    o_ref[...] = jnp.concatenate(heads, axis=0).T


def _head_rows(e):
    return slice(e * HEAD_DIM, (e + 1) * HEAD_DIM)


def _lane_half(hq):
    return slice(hq * LANES, (hq + 1) * LANES)


def _sweep_scratch(tq):
    nchain = 2 * (tq // LANES)
    return [pltpu.VMEM((3, nchain, MOBA_BLOCK, LANES), F32),
            pltpu.VMEM((2, nchain, MOBA_BLOCK, LANES), F32), pltpu.VMEM((2, 2, MOBA_BLOCK, tq), BF16),
            pltpu.VMEM((nchain, HEAD_DIM, LANES), F32)]


def _rel_pos(hq):
    return (lax.broadcasted_iota(jnp.int32, (MOBA_BLOCK, LANES), 1) + hq * LANES
            - lax.broadcasted_iota(jnp.int32, (MOBA_BLOCK, LANES), 0))


def _moba_kernel(slopes_ref, q_ref, k_ref, v_ref, o_ref, kb_s, vt_s, kmean_s, sel_s, bias_s, qk_s, p_s, acc_s):
    hp = pl.program_id(1)
    own = pl.program_id(2)
    tq = q_ref.shape[0]
    nblk = kmean_s.shape[0]
    nhq = tq // LANES
    slopes2 = [slopes_ref[hp * 2 + e] * LOG2E for e in range(2)]

    @pl.when(own == 0)
    def _():
        _stage_kv(k_ref, v_ref, kb_s, vt_s, kmean_s)
        for e in range(2):
            for hq in range(nhq):
                rel = _rel_pos(hq)
                alibi = slopes2[e] * rel.astype(F32)
                bias_s[PAST, e * nhq + hq] = alibi
                bias_s[OWN, e * nhq + hq] = jnp.where(rel >= 0, alibi, BIG)
                bias_s[DEAD, e * nhq + hq] = jnp.full(alibi.shape, BIG, F32)

    q = q_ref[...]
    lane_head = _half(lax.broadcasted_iota(jnp.int32, (1, PAIR), 1))
    blk = lax.broadcasted_iota(jnp.int32, (nblk, tq), 0)
    qs = []
    for e in range(2):
        qm = jnp.where(lane_head == e, q, 0.0)
        gate = _dot_nt(kmean_s[...], qm, precision=lax.Precision.HIGHEST)
        valid = blk < own
        gate = jnp.where(valid, gate, -jnp.inf)
        rank = jnp.zeros((nblk, tq), F32)
        for mth in range(nblk):
            g_m = gate[mth:mth + 1, :]
            beats = jnp.where(g_m > gate, 1.0, jnp.where((g_m == gate) & (blk > mth), 1.0, 0.0))
            rank = rank + beats
        sel = jnp.where((valid & (rank < MOBA_TOPK)) | (blk == own), 1.0, 0.0)
        for mth in range(nblk):
            sel_s[e, mth] = jnp.broadcast_to(sel[mth:mth + 1, :], sel_s.shape[2:])
        qe = (qm * (HEAD_DIM ** -0.5 * LOG2E)).astype(BF16)
        qs.append(qe)

    def row_fn(e, hq, b, off):
        picked = sel_s[e, b][0:1, _lane_half(hq)]
        return (picked - 1.0) * BIG - slopes2[e] * off.astype(F32)

    _flash_sweep(own, qs, lambda e, b: kb_s[_block_rows(b), :], vt_s, bias_s, row_fn, qk_s, p_s, acc_s, o_ref)


def _moba_prompt(slopes, q, k, v, batch, seq):
    n, width = q.shape
    tq = MOBA_BLOCK
    nq = seq // tq
    npair = width // PAIR
    return pl.pallas_call(
        _moba_kernel,
        grid=(batch, npair, nq),
        in_specs=[pl.BlockSpec(memory_space=pltpu.SMEM),
                  pl.BlockSpec((tq, PAIR), lambda b, h, i: (b * nq + i, h)),
                  pl.BlockSpec((seq, PAIR), lambda b, h, i: (b, h)),
                  pl.BlockSpec((seq, PAIR), lambda b, h, i: (b, h))],
        out_specs=pl.BlockSpec((tq, PAIR), lambda b, h, i: (b * nq + i, h)),
        out_shape=jax.ShapeDtypeStruct((n, width), F32),
        scratch_shapes=[pltpu.VMEM((seq, PAIR), BF16), pltpu.VMEM((PAIR, seq), BF16),
                        pltpu.VMEM((seq // MOBA_BLOCK, PAIR), F32),
                        pltpu.VMEM((2, seq // MOBA_BLOCK, SUBLANES, tq), F32)] + _sweep_scratch(tq),
        compiler_params=_params("parallel", "parallel", "arbitrary"),
        name="moba",
    )(slopes, q, k, v)


def _split3(x):
    hi = x.astype(BF16).astype(F32)
    mid = (x - hi).astype(BF16).astype(F32)
    lo = ((x - hi) - mid).astype(BF16).astype(F32)
    return hi, mid, lo


def _with_bias_lanes(x, e, ones_at, column, column_at):
    lane = lax.broadcasted_iota(jnp.int32, (1, PAIR), 1)
    base = (1 - e) * HEAD_DIM
    out = jnp.where(_half(lane) == e, x, 0.0)
    out = jnp.where((lane >= base + ones_at) & (lane < base + ones_at + 3), 1.0, out)
    for i, part in enumerate(_split3(column)):
        out = jnp.where(lane == base + column_at + i, part, out)
    return out


def _fox_kernel(q_ref, k_ref, v_ref, c_ref, o_ref, kb_s, vt_s, bias_s, qk_s, p_s, acc_s):
    own = pl.program_id(2)
    tq = q_ref.shape[0]
    nhq = tq // LANES

    @pl.when(own == 0)
    def _():
        _stage_kv(k_ref, v_ref, None, vt_s, None)
        for blk in range(k_ref.shape[0] // MOBA_BLOCK):
            rows = slice(blk * MOBA_BLOCK, (blk + 1) * MOBA_BLOCK)
            for e in range(2):
                kb_s[e, rows, :] = _with_bias_lanes(k_ref[rows, :], e, 3, c_ref[rows, e:e + 1] * -LOG2E, 0).astype(BF16)
        for c in range(2 * nhq):
            bias_s[PAST, c] = jnp.zeros(bias_s.shape[2:], F32)
            bias_s[OWN, c] = jnp.where(_rel_pos(c % nhq) >= 0, 0.0, BIG)
            bias_s[DEAD, c] = jnp.full(bias_s.shape[2:], BIG, F32)

    q = q_ref[...] * (HEAD_DIM ** -0.5 * LOG2E)
    qs = [_with_bias_lanes(q, e, 0, c_ref[_block_rows(own), e:e + 1] * LOG2E, 3).astype(BF16) for e in range(2)]
    _flash_sweep(own, qs, lambda e, b: kb_s[e, _block_rows(b), :], vt_s, bias_s, None, qk_s, p_s, acc_s, o_ref)


def _fox_prompt(q, k, v, c, batch, seq):
    n, width = q.shape
    tq = MOBA_BLOCK
    nq = seq // tq
    npair = width // PAIR
    return pl.pallas_call(
        _fox_kernel,
        grid=(batch, npair, nq),
        in_specs=[pl.BlockSpec((tq, PAIR), lambda b, h, i: (b * nq + i, h)),
                  pl.BlockSpec((seq, PAIR), lambda b, h, i: (b, h)),
                  pl.BlockSpec((seq, PAIR), lambda b, h, i: (b, h)),
                  pl.BlockSpec((None, None, seq, 2), lambda b, h, i: (b, h, 0, 0))],
        out_specs=pl.BlockSpec((tq, PAIR), lambda b, h, i: (b * nq + i, h)),
        out_shape=jax.ShapeDtypeStruct((n, width), F32),
        scratch_shapes=[pltpu.VMEM((2, seq, PAIR), BF16), pltpu.VMEM((PAIR, seq), BF16)] + _sweep_scratch(tq),
        compiler_params=_params("parallel", "parallel", "arbitrary"),
        name="fox",
    )(q, k, v, c)


def _merge_kernel(x_ref, oa_ref, ob_ref, ga_ref, gb_ref, wa_ref, wb_ref, wo_ref, g_ref, wq_ref, sk_ref,
                  h_ref, hn_ref, st_ref):
    a = _dot(oa_ref[...].astype(BF16), wa_ref[...])
    b = _dot(ob_ref[...].astype(BF16), wb_ref[...])
    mix = ga_ref[...] * a + gb_ref[...] * b
    h = x_ref[...] + _dot(mix.astype(BF16), wo_ref[...])
    h_ref[...] = h
    hn = _rms(h, g_ref[...]).astype(BF16)
    hn_ref[...] = hn
    for i in range(0, st_ref.shape[0], 2):
        qp = _dot(hn, wq_ref[:, i * PEER_NKEYS:(i + 2) * PEER_NKEYS]).astype(BF16)
        for j in range(2):
            st_ref[i + j] = _dot_nt(sk_ref[i + j], qp[:, j * PEER_NKEYS:(j + 1) * PEER_NKEYS])


def _merge(x, oa, ob, ga, gb, wa, wb, wo, g, wq, sk, tm):
    n, d = x.shape
    width = oa.shape[1]
    nsk = sk.shape[0]
    row = lambda i: (i, 0)
    fix = lambda i: (0, 0)
    return pl.pallas_call(
        _merge_kernel,
        grid=(n // tm,),
        in_specs=[pl.BlockSpec((tm, d), row), pl.BlockSpec((tm, width), row), pl.BlockSpec((tm, width), row),
                  pl.BlockSpec((tm, d), row), pl.BlockSpec((tm, d), row),
                  pl.BlockSpec(wa.shape, fix), pl.BlockSpec(wb.shape, fix), pl.BlockSpec(wo.shape, fix),
                  pl.BlockSpec((1, d), fix), pl.BlockSpec(wq.shape, fix),
                  pl.BlockSpec(sk.shape, lambda i: (0, 0, 0))],
        out_specs=[pl.BlockSpec((tm, d), row), pl.BlockSpec((tm, d), row),
                   pl.BlockSpec((nsk, PEER_NKEYS, tm), lambda i: (0, 0, i))],
        out_shape=[jax.ShapeDtypeStruct((n, d), F32), jax.ShapeDtypeStruct((n, d), BF16),
                   jax.ShapeDtypeStruct((nsk, PEER_NKEYS, n), F32)],
        compiler_params=_params("parallel"),
        name="merge",
    )(x, oa, ob, ga, gb, wa, wb, wo, g, wq, sk)


def _cand_rows():
    rows = []
    for j in range(PEER_TOPK):
        rows.append(PEER_TOPK // (j + 1))
    return rows


def _extract_sorted(vals, count, dst_ref=None):
    sub = lax.broadcasted_iota(jnp.int32, vals.shape, 0).astype(F32)
    out = []
    for r in range(count):
        mx = jnp.max(vals, axis=0, keepdims=True)
        out.append(mx)
        if dst_ref is not None:
            dst_ref[r:r + 1, :] = mx
        first = jnp.min(jnp.where(vals == mx, sub, float(vals.shape[0])), axis=0, keepdims=True)
        vals = jnp.where(sub == first, -jnp.inf, vals)
    return out


def _batcher_network(n):
    def merge(lo, hi, r):
        step = r * 2
        if step < hi - lo:
            yield from merge(lo, hi, step)
            yield from merge(lo + r, hi, step)
            yield from [(i, i + r) for i in range(lo + r, hi - r, step)]
        else:
            yield (lo, lo + r)

    def sort(lo, hi):
        if hi - lo >= 1:
            mid = lo + (hi - lo) // 2
            yield from sort(lo, mid)
            yield from sort(mid + 1, hi)
            yield from merge(lo, hi, 1)

    return list(sort(0, n - 1))


def _top_sorted(vals, count, dst_ref):
    rows = vals.shape[0]
    ngrp = rows // SUBLANES
    assert ngrp >= count and ngrp & (ngrp - 1) == 0
    v = [vals[k * SUBLANES:(k + 1) * SUBLANES, :] for k in range(ngrp)]
    for i, j in _batcher_network(ngrp):
        v[i], v[j] = jnp.maximum(v[i], v[j]), jnp.minimum(v[i], v[j])
    sub = lax.broadcasted_iota(jnp.int32, v[0].shape, 0).astype(F32)
    for r in range(count):
        mx = jnp.max(v[0], axis=0, keepdims=True)
        dst_ref[r:r + 1, :] = mx
        first = jnp.min(jnp.where(v[0] == mx, sub, float(SUBLANES)), axis=0, keepdims=True)
        popped = sub == first
        for k in range(count - 1 - r):
            v[k] = jnp.where(popped, v[k + 1], v[k])


def _bf16_twice(x):
    bits = pltpu.bitcast(x.astype(BF16).astype(F32), jnp.uint32)
    return bits | lax.shift_right_logical(bits, jnp.uint32(16))


def _select_head(h, st_ref, rank2_ref, f2_ref, cnt_ref, c1_ref, a_s, b_s, cand_s):
    rows = _cand_rows()
    s1 = st_ref[2 * h]
    s2 = st_ref[2 * h + 1]
    _top_sorted(s1, PEER_TOPK, a_s)
    _top_sorted(s2, PEER_TOPK, b_s)
    a = a_s[...]
    b = b_s[...]
    cand_s[...] = jnp.full(cand_s.shape, -jnp.inf, F32)
    off = 0
    for j, lim in enumerate(rows):
        cand_s[off:off + lim, :] = a[0:lim, :] + b[j:j + 1, :]
        off += lim
    tops = _extract_sorted(cand_s[...], PEER_TOPK)
    theta = tops[PEER_TOPK - 1]
    z = jnp.ones_like(theta)
    for t in tops[1:]:
        z = z + jnp.exp(t - tops[0])
    cnt = jnp.zeros(s1.shape, F32)
    rank2 = jnp.zeros(s2.shape, F32)
    for j in range(PEER_TOPK):
        bj = b[j:j + 1, :]
        cnt = cnt + jnp.where(s1 + bj >= theta, 1.0, 0.0)
        rank2 = rank2 + jnp.where(bj > s2, 1.0, 0.0)
    cnt_ref[h] = _bf16_twice(cnt)
    rank2_ref[h] = pltpu.bitcast(rank2.astype(BF16), jnp.uint32)
    f2_ref[h] = pltpu.bitcast(jnp.exp(s2 - b[0:1, :]).astype(BF16), jnp.uint32)
    c1_ref[h] = _bf16_twice(jnp.exp(s1 - a[0:1, :]) / z)


def _select_kernel(st_ref, rank2_ref, f2_ref, cnt_ref, c1_ref, *scratch):
    def pair(hp, carry):
        for e in range(2):
            _select_head(2 * hp + e, st_ref, rank2_ref, f2_ref, cnt_ref, c1_ref, *scratch[3 * e:3 * e + 3])
        return carry

    lax.fori_loop(0, PEER_HEADS // 2, pair, 0)


def _select(st, tl=LANES):
    nsk, nk, n = st.shape
    spec_in = pl.BlockSpec((nsk, nk, tl), lambda i: (0, 0, i))
    spec_out = lambda rows: pl.BlockSpec((PEER_HEADS, rows, tl), lambda i: (0, 0, i))
    ncand = -(-sum(_cand_rows()) // SUBLANES) * SUBLANES
    scratch = [pltpu.VMEM((PEER_TOPK, tl), F32), pltpu.VMEM((PEER_TOPK, tl), F32), pltpu.VMEM((ncand, tl), F32)]
    shape = lambda rows: jax.ShapeDtypeStruct((PEER_HEADS, rows, n), jnp.uint32)
    return pl.pallas_call(
        _select_kernel,
        grid=(n // tl,),
        in_specs=[spec_in],
        out_specs=[spec_out(nk // 2), spec_out(nk // 2), spec_out(nk), spec_out(nk)],
        out_shape=[shape(nk // 2), shape(nk // 2), shape(nk), shape(nk)],
        scratch_shapes=scratch * 2,
        compiler_params=_params("parallel"),
        name="select",
    )(st)


def _gelu(x):
    return 0.5 * x * (1.0 + jnp.tanh(0.7978845608028654 * (x + 0.044715 * (x * x * x))))


def _peer_weights(il, tc, g0, rank2_ref, f2_ref, cnt_ref, c1_ref):
    cols = slice(tc * LANES, (tc + 1) * LANES)
    cnt, c1 = [], []
    for h in range(PEER_HEADS):
        grp = pl.ds(pl.multiple_of(h * PEER_NKEYS + g0, SUBLANES), SUBLANES)
        for ref, dst in ((cnt_ref, cnt), (c1_ref, c1)):
            row = jnp.broadcast_to(ref[grp, cols][il:il + 1, :], (SUBLANES, LANES))
            dst.append(pltpu.bitcast(row, BF16))
    slabs = []
    for r in range(PEER_NKEYS // BF16_ROWS):
        w = jnp.zeros((BF16_ROWS, LANES), BF16)
        for h in range(PEER_HEADS):
            words = slice((h * PEER_NKEYS + r * BF16_ROWS) // 2, (h * PEER_NKEYS + (r + 1) * BF16_ROWS) // 2)
            rank2 = pltpu.bitcast(rank2_ref[words, cols], BF16)
            f2 = pltpu.bitcast(f2_ref[words, cols], BF16)
            w = w + jnp.where(rank2 < cnt[h], f2 * c1[h], 0.0)
        slabs.append(w)
    return slabs


def _peer_kernel(hn_ref, h_ref, rank2_ref, f2_ref, cnt_ref, c1_ref, u_ref, vtp_ref, vtc_ref, g_ref, y_ref,
                 acc_s, s_s, pa_s, pb_s):
    ec = pl.program_id(1)
    last = pl.num_programs(1) - 1
    ecn, tb = pa_s.shape
    n_i1 = ecn // PEER_NKEYS
    n_tc = tb // LANES
    assert n_i1 == SUBLANES
    n_half = 2 if n_tc % 2 == 0 else 1
    halves = [slice(k * (tb // n_half), (k + 1) * (tb // n_half)) for k in range(n_half)]

    @pl.when(ec == 0)
    def _():
        acc_s[...] = jnp.zeros_like(acc_s)
        pb_s[...] = jnp.zeros_like(pb_s)

    def step(p_prev, p_cur):
        g0 = pl.multiple_of(ec * n_i1, SUBLANES)
        for cols in halves:
            s_s[:, cols] = _dot_nt(u_ref[...], hn_ref[cols, :])
        acc_s[...] += _dot(vtp_ref[...], p_prev[...])
        for tc in range(n_tc):
            cols = slice(tc * LANES, (tc + 1) * LANES)
            for il in range(n_i1):
                for r, w in enumerate(_peer_weights(il, tc, g0, rank2_ref, f2_ref, cnt_ref, c1_ref)):
                    rows = slice(il * PEER_NKEYS + r * BF16_ROWS, il * PEER_NKEYS + (r + 1) * BF16_ROWS)
                    p_cur[rows, cols] = w * _gelu(s_s[rows, cols].astype(BF16))

        @pl.when(ec == last)
        def _():
            y_ref[...] = _rms(h_ref[...] + (acc_s[...] + _dot(vtc_ref[...], p_cur[...])).T, g_ref[...])

    pl.when(ec % 2 == 0)(lambda: step(pb_s, pa_s))
    pl.when(ec % 2 == 1)(lambda: step(pa_s, pb_s))


def _peer(hn, h, rank2, f2, cnt, c1, u, vt, g, tb, ecn):
    n, d = h.shape
    nchunk = u.shape[0] // ecn
    assert nchunk >= 2
    tok = lambda t, e: (t, 0)
    tables = [z.reshape(-1, n) for z in (rank2, f2, cnt, c1)]
    sel = [pl.BlockSpec((z.shape[0], tb), lambda t, e: (0, t)) for z in tables]
    rank2, f2, cnt, c1 = tables
    return pl.pallas_call(
        _peer_kernel,
        grid=(n // tb, nchunk),
        in_specs=[pl.BlockSpec((tb, d), tok), pl.BlockSpec((tb, d), tok)] + sel + [
                  pl.BlockSpec((ecn, d), lambda t, e: (e, 0)),
                  pl.BlockSpec((d, ecn), lambda t, e: (0, jnp.maximum(e - 1, 0))),
                  pl.BlockSpec((d, ecn), lambda t, e: (0, (e // (nchunk - 1)) * (nchunk - 1))),
                  pl.BlockSpec((1, d), lambda t, e: (0, 0))],
        out_specs=pl.BlockSpec((tb, d), tok),
        out_shape=jax.ShapeDtypeStruct((n, d), F32),
        scratch_shapes=[pltpu.VMEM((d, tb), F32), pltpu.VMEM((ecn, tb), F32),
                        pltpu.VMEM((ecn, tb), BF16), pltpu.VMEM((ecn, tb), BF16)],
        compiler_params=_params("parallel", "arbitrary"),
        name="peer",
    )(hn, h, rank2, f2, cnt, c1, u, vt, vt, g)


def _page_index(per_step, i):
    def index(s, p, pt):
        return (0, pt[s, p * per_step + i], 0, 0, 0)
    return index


def _gather_pages_kernel(pt_ref, *refs):
    o_ref = refs[-1]
    for i in range(len(refs) - 1):
        o_ref[i] = refs[i][...]


def _gather_pages(page_table, cache_t, per_step=32):
    nseq, npages = page_table.shape
    heads, page = cache_t.shape[2:]
    per_step = min(per_step, npages)
    grid_spec = pltpu.PrefetchScalarGridSpec(
        num_scalar_prefetch=1,
        grid=(nseq, npages // per_step),
        in_specs=[pl.BlockSpec((None, None, heads, page),
                               (lambda i: lambda s, p, pt: (0, pt[s, p * per_step + i], 0, 0))(i))
                  for i in range(per_step)],
        out_specs=pl.BlockSpec((None, per_step, heads, page), lambda s, p, pt: (s, p, 0, 0)),
    )
    return pl.pallas_call(
        _gather_pages_kernel, grid_spec=grid_spec,
        out_shape=jax.ShapeDtypeStruct((nseq, npages, heads, page), F32),
        compiler_params=_params("parallel", "arbitrary"),
        name="gather_logf",
    )(page_table, *([cache_t] * per_step))


def _fox_sample_kernel(pt_ref, q_ref, kn_ref, vn_ref, cq_ref, ck_ref, *refs):
    n_pg = (len(refs) - 4) // 2
    k_refs = refs[:n_pg]
    v_refs = refs[n_pg:2 * n_pg]
    o_ref, m_s, l_s, acc_s = refs[2 * n_pg:]
    step = pl.program_id(1)
    heads = q_ref.shape[0]

    @pl.when(step == 0)
    def _():
        m_s[...] = jnp.full(m_s.shape, NEG, F32)
        l_s[...] = jnp.zeros_like(l_s)
        acc_s[...] = jnp.zeros_like(acc_s)

    for h in range(heads):
        q = q_ref[h] * HEAD_DIM ** -0.5
        bias_q = cq_ref[h:h + 1, :]
        lgs = [jnp.sum(q * k_refs[i][h], axis=0, keepdims=True) + (bias_q - ck_ref[i, h:h + 1, :])
               for i in range(n_pg)]
        top = lgs[0]
        for lg in lgs[1:]:
            top = jnp.maximum(top, lg)
        m = m_s[h:h + 1, :]
        mn = jnp.maximum(m, jnp.max(top, axis=1, keepdims=True))
        alpha = jnp.exp(m - mn)
        l = alpha * l_s[h:h + 1, :]
        acc = alpha * acc_s[h]
        for i in range(n_pg):
            p = jnp.exp(lgs[i] - mn)
            l = l + p
            acc = acc + p * v_refs[i][h]
        m_s[h:h + 1, :] = mn
        l_s[h:h + 1, :] = l
        acc_s[h] = acc

    @pl.when(step == pl.num_programs(1) - 1)
    def _():
        for h in range(heads):
            q = q_ref[h] * HEAD_DIM ** -0.5
            lg_new = jnp.sum(q * kn_ref[h], axis=0, keepdims=True)
            m = m_s[h:h + 1, :]
            mf = jnp.maximum(m, lg_new)
            a = jnp.exp(m - mf)[:, 0:1]
            b = jnp.exp(lg_new - mf)[:, 0:1]
            num = a * jnp.sum(acc_s[h], axis=1, keepdims=True) + b * vn_ref[h][:, 0:1]
            den = a * jnp.sum(l_s[h:h + 1, :], axis=1, keepdims=True) + b
            o_ref[h] = num / den


def _fox_sample(page_table, qb, knb, vnb, cq, ck, cache_k, cache_v, per_step=16):
    nseq, npages = page_table.shape
    heads, hd, page = cache_k.shape[2:]
    per_step = min(per_step, npages)
    tok = pl.BlockSpec((None, heads, hd, LANES), lambda s, p, pt: (s, 0, 0, 0))
    pages = [pl.BlockSpec((None, None, heads, hd, page), _page_index(per_step, i)) for i in range(per_step)]
    grid_spec = pltpu.PrefetchScalarGridSpec(
        num_scalar_prefetch=1,
        grid=(nseq, npages // per_step),
        in_specs=[tok, tok, tok,
                  pl.BlockSpec((None, heads, LANES), lambda s, p, pt: (s, 0, 0)),
                  pl.BlockSpec((None, per_step, heads, page), lambda s, p, pt: (s, p, 0, 0))] + pages + pages,
        out_specs=pl.BlockSpec((None, heads, hd, 1), lambda s, p, pt: (s, 0, 0, 0)),
        scratch_shapes=[pltpu.VMEM((heads, LANES), F32), pltpu.VMEM((heads, LANES), F32),
                        pltpu.VMEM((heads, hd, LANES), F32)],
    )
    return pl.pallas_call(
        _fox_sample_kernel, grid_spec=grid_spec,
        out_shape=jax.ShapeDtypeStruct((nseq, heads, hd, 1), F32),
        compiler_params=_params("parallel", "arbitrary"),
        name="fox_sample",
    )(page_table, qb, knb, vnb, cq, ck, *([cache_k] * per_step), *([cache_v] * per_step))


def _moba_gate_kernel(pt_ref, q_ref, *refs):
    n_pg = len(refs) - 2
    k_refs = refs[:n_pg]
    sel_ref, gate_s = refs[n_pg:]
    step = pl.program_id(1)
    heads, _, page = k_refs[0].shape
    pages_per_block = MOBA_BLOCK // page
    blocks_per_step = n_pg // pages_per_block
    for h in range(heads):
        q = q_ref[h]
        for i in range(blocks_per_step):
            t = jnp.sum(q * k_refs[i * pages_per_block][h], axis=0, keepdims=True)
            for j in range(1, pages_per_block):
                t = t + jnp.sum(q * k_refs[i * pages_per_block + j][h], axis=0, keepdims=True)
            gate = jnp.sum(t, axis=1, keepdims=True) * (1.0 / MOBA_BLOCK)
            gate_s[step * blocks_per_step + i, h:h + 1, :] = jnp.broadcast_to(gate, (1, LANES))

    @pl.when(step == pl.num_programs(1) - 1)
    def _():
        g = gate_s[...]
        blk = lax.broadcasted_iota(jnp.int32, g.shape, 0)
        for r in range(MOBA_TOPK):
            mx = jnp.max(g, axis=0, keepdims=True)
            first = jnp.min(jnp.where(g == mx, blk, g.shape[0]), axis=0, keepdims=True)
            sel_ref[r] = first[0]
            g = jnp.where(blk == first, -jnp.inf, g)


def _moba_gate(page_table, qb, cache_k, per_step=16):
    nseq, npages = page_table.shape
    heads, hd, page = cache_k.shape[2:]
    per_step = min(per_step, npages)
    nblk = npages * page // MOBA_BLOCK
    grid_spec = pltpu.PrefetchScalarGridSpec(
        num_scalar_prefetch=1,
        grid=(nseq, npages // per_step),
        in_specs=[pl.BlockSpec((None, heads, hd, LANES), lambda s, p, pt: (s, 0, 0, 0))]
        + [pl.BlockSpec((None, None, heads, hd, page), _page_index(per_step, i)) for i in range(per_step)],
        out_specs=pl.BlockSpec((None, MOBA_TOPK, heads, LANES), lambda s, p, pt: (s, 0, 0, 0)),
        scratch_shapes=[pltpu.VMEM((nblk, heads, LANES), F32)],
    )
    return pl.pallas_call(
        _moba_gate_kernel, grid_spec=grid_spec,
        out_shape=jax.ShapeDtypeStruct((nseq, MOBA_TOPK, heads, LANES), jnp.int32),
        compiler_params=_params("parallel", "arbitrary"),
        name="moba_gate",
    )(page_table, qb, *([cache_k] * per_step))


def _moba_sample_kernel(phys_ref, start_ref, slopes_ref, q_ref, kn_ref, vn_ref, *refs, past):
    n_pg = (len(refs) - 1) // 2
    k_refs = refs[:n_pg]
    v_refs = refs[n_pg:2 * n_pg]
    o_ref = refs[-1]
    s_id, h = pl.program_id(0), pl.program_id(1)
    heads = pl.num_programs(1)
    q = q_ref[...] * HEAD_DIM ** -0.5
    slope = slopes_ref[h]
    lane = lax.broadcasted_iota(jnp.int32, (1, k_refs[0].shape[-1]), 1)
    lg_new = jnp.sum(q * kn_ref[...], axis=0, keepdims=True)
    lgs = []
    top = None
    for i in range(n_pg):
        start = start_ref[(s_id * heads + h) * n_pg + i]
        dist = (past - start - lane).astype(F32)
        lg = jnp.sum(q * k_refs[i][...], axis=0, keepdims=True) - slope * dist
        lgs.append(lg)
        top = lg if top is None else jnp.maximum(top, lg)
    m = jnp.maximum(lg_new, jnp.max(top, axis=1, keepdims=True))
    b = jnp.exp(lg_new - m)
    l = jnp.zeros_like(top)
    acc = jnp.zeros(q.shape, F32)
    for i in range(n_pg):
        p = jnp.exp(lgs[i] - m)
        l = l + p
        acc = acc + p * v_refs[i][...]
    num = jnp.sum(acc, axis=1, keepdims=True) + b[:, 0:1] * vn_ref[...][:, 0:1]
    den = jnp.sum(l, axis=1, keepdims=True) + b[:, 0:1]
    o_ref[...] = num / den


def _moba_sample(phys, start, slopes, qb, knb, vnb, cache_k, cache_v, past):
    nseq, heads, hd, _ = qb.shape
    page = cache_k.shape[-1]
    n_pg = phys.shape[0] // (nseq * heads)
    tok = pl.BlockSpec((None, None, hd, LANES), lambda s, h, ph, st: (s, h, 0, 0))

    def pg(i):
        return pl.BlockSpec((None, None, None, hd, page),
                            lambda s, h, ph, st: (0, ph[(s * heads + h) * n_pg + i], h, 0, 0))

    pages = [pg(i) for i in range(n_pg)]
    grid_spec = pltpu.PrefetchScalarGridSpec(
        num_scalar_prefetch=2,
        grid=(nseq, heads),
        in_specs=[pl.BlockSpec(memory_space=pltpu.SMEM), tok, tok, tok] + pages + pages,
        out_specs=pl.BlockSpec((None, None, hd, 1), lambda s, h, ph, st: (s, h, 0, 0)),
    )
    return pl.pallas_call(
        functools.partial(_moba_sample_kernel, past=past), grid_spec=grid_spec,
        out_shape=jax.ShapeDtypeStruct((nseq, heads, hd, 1), F32),
        compiler_params=_params("parallel", "arbitrary"),
        name="moba_sample",
    )(phys, start, slopes, qb, knb, vnb, *([cache_k] * n_pg), *([cache_v] * n_pg))


def _pad_rows(x, rows):
    return jnp.pad(x, ((0, rows - x.shape[0]),) + ((0, 0),) * (x.ndim - 1))


def _layer_weights(l, g_mix, w_in, b_forget, w_branch_a, w_branch_b, w_out, g_ffn, w_peer_q, peer_sub_keys,
                   peer_u, peer_v):
    d = w_in.shape[1]
    wa_w = w_branch_a.shape[1]
    n_qkv = 6 * wa_w
    w = w_in[l]
    wf = jnp.pad(w[:, n_qkv:n_qkv + N_HEADS], ((0, 0), (0, LANES - N_HEADS)))
    wfh = wf.astype(BF16)
    wfl = (wf - wfh.astype(F32)).astype(BF16)
    sk = peer_sub_keys[l]
    wkvt = jnp.concatenate([w[:, i * wa_w:(i + 1) * wa_w] for i in (1, 2, 4, 5)], axis=1).T.astype(BF16)
    return dict(
        g_mix=g_mix[l].reshape(1, d), wqkv=w[:, :n_qkv].astype(BF16), wfh=wfh, wfl=wfl, wkvt=wkvt,
        bf=b_forget[l].reshape(1, N_HEADS), wg=w[:, n_qkv + N_HEADS:].astype(BF16),
        wa=w_branch_a[l].astype(BF16), wb=w_branch_b[l].astype(BF16), wo=w_out[l].astype(BF16),
        g_ffn=g_ffn[l].reshape(1, d), wq=w_peer_q[l].astype(BF16),
        sk=sk.reshape((sk.shape[0] * sk.shape[1],) + sk.shape[2:]).astype(BF16),
        u=peer_u[l].astype(BF16), vt=peer_v[l].T.astype(BF16))


def _ffn(x, oa, ob, ga, gb, wts, g_out, tm, tb, ecn):
    h, hn, st = _merge(x, oa, ob, ga, gb, wts["wa"], wts["wb"], wts["wo"], wts["g_ffn"], wts["wq"], wts["sk"], tm)
    rank2, f2, cnt, c1 = _select(st)
    return _peer(hn, h, rank2, f2, cnt, c1, wts["u"], wts["vt"], g_out, tb, ecn)


def kernel(x_prompt, x_sample, cache_moba_k, cache_moba_v, cache_fox_k, cache_fox_v, cache_fox_logf, page_table,
           g_mix, w_in, b_forget, w_branch_a, w_branch_b, w_out, g_ffn, w_peer_q, peer_sub_keys, peer_u, peer_v,
           g_final):
    batch, seq, d = x_prompt.shape
    nseq, dec_seq, _ = x_sample.shape
    depth = w_in.shape[0]
    assert depth == 1 and dec_seq == 1, "one layer, one new token per sampled sequence"
    npages, page = page_table.shape[1], cache_moba_k.shape[2]
    past = npages * page
    assert past % MOBA_BLOCK == 0 and past // MOBA_BLOCK >= MOBA_TOPK and seq % MOBA_BLOCK == 0
    assert page == LANES
    slopes = 2.0 ** (-8.0 * jnp.arange(1, N_HEADS + 1, dtype=F32) / N_HEADS)
    wts = _layer_weights(0, g_mix, w_in, b_forget, w_branch_a, w_branch_b, w_out, g_ffn, w_peer_q, peer_sub_keys,
                         peer_u, peer_v)
    g_out = g_final.reshape(1, d)
    proj_w = (wts["g_mix"], wts["wqkv"], wts["wfh"], wts["wfl"], wts["bf"], wts["wg"], wts["wkvt"])
    npair = N_HEADS // 2
    ecn = SUBLANES * PEER_NKEYS

    n = batch * seq
    xp = x_prompt.reshape(n, d)
    qa, ka, va, qb, kb, vb, lf, ga, gb, *kv_t = _proj(xp, *proj_w, tm=min(256, seq), batch=batch)
    c = _cumsum_rows(lf.reshape(batch, seq, N_HEADS).transpose(1, 0, 2).reshape(seq, batch * N_HEADS))
    c = c.reshape(seq, batch, npair, 2)
    oa = _moba_prompt(slopes, qa, ka, va, batch, seq)
    ob = _fox_prompt(qb, kb, vb, c.transpose(1, 2, 0, 3), batch, seq)
    y_prompt = _ffn(xp, oa, ob, ga, gb, wts, g_out, min(256, n), min(512, n), ecn).reshape(batch, seq, d)
    p_new = tuple(z.reshape(1, batch, N_HEADS, HEAD_DIM, seq).transpose(0, 1, 4, 2, 3) for z in kv_t) + (
        lf.reshape(1, batch, seq, N_HEADS),)

    ns = -(-nseq // LANES) * LANES
    xs = _pad_rows(x_sample.reshape(nseq, d), ns)
    qa, ka, va, qb, kb, vb, lf, ga, gb = _proj(xs, *proj_w, tm=ns)
    lane_bcast = lambda z: jnp.broadcast_to(z[:nseq].reshape(nseq, N_HEADS, HEAD_DIM, 1),
                                            (nseq, N_HEADS, HEAD_DIM, LANES))
    pages_minor = lambda cache: cache.transpose(0, 1, 3, 4, 2)
    mk, mv = pages_minor(cache_moba_k), pages_minor(cache_moba_v)
    sel = _moba_gate(page_table, lane_bcast(qa), mk)[..., 0]
    ppb = MOBA_BLOCK // page
    sel_pages = sel.transpose(0, 2, 1)[..., None] * ppb + jnp.arange(ppb, dtype=jnp.int32)
    sel_pages = sel_pages.reshape(nseq, N_HEADS * MOBA_TOPK * ppb)
    phys = jnp.take_along_axis(page_table, sel_pages, axis=1).reshape(-1)
    start = (sel_pages * page).reshape(-1)
    oa = _moba_sample(phys, start, slopes, lane_bcast(qa), lane_bcast(ka), lane_bcast(va), mk, mv, past)
    lf_past = _gather_pages(page_table, cache_fox_logf.transpose(0, 1, 3, 2))
    lf_past = lf_past.transpose(0, 1, 3, 2).reshape(nseq, past, N_HEADS)
    lf_all = jnp.concatenate([lf_past, lf[:nseq, None, :]], axis=1)
    tot = -(-(past + 1) // MOBA_BLOCK) * MOBA_BLOCK
    lf_all = jnp.pad(lf_all, ((0, 0), (0, tot - past - 1), (0, 0)))
    cs = _cumsum_rows(lf_all.transpose(1, 0, 2).reshape(tot, nseq * N_HEADS)).reshape(tot, nseq, N_HEADS)
    cq = jnp.broadcast_to(cs[past][:, :, None], (nseq, N_HEADS, LANES))
    ck = cs[:past].reshape(npages, page, nseq, N_HEADS).transpose(2, 0, 3, 1)
    ob = _fox_sample(page_table, lane_bcast(qb), lane_bcast(kb), lane_bcast(vb), cq, ck,
                     pages_minor(cache_fox_k), pages_minor(cache_fox_v))
    oa = _pad_rows(oa.reshape(nseq, N_HEADS * HEAD_DIM), ns)
    ob = _pad_rows(ob.reshape(nseq, N_HEADS * HEAD_DIM), ns)
    y_sample = _ffn(xs, oa, ob, ga, gb, wts, g_out, ns, ns, ecn)[:nseq].reshape(nseq, 1, d)
    kv_shape = (1, nseq, 1, N_HEADS, HEAD_DIM)
    s_new = (ka[:nseq].reshape(kv_shape), va[:nseq].reshape(kv_shape), kb[:nseq].reshape(kv_shape),
             vb[:nseq].reshape(kv_shape), lf[:nseq].reshape(1, nseq, 1, N_HEADS))
    return (y_prompt, y_sample) + p_new + s_new
```

```python
import functools

import jax
import jax.numpy as jnp
from jax import lax
from jax.experimental import pallas as pl
from jax.experimental.pallas import tpu as pltpu

F32 = jnp.float32
BF16 = jnp.bfloat16

HEAD_DIM = 64
N_HEADS = 8
PAIR = 2 * HEAD_DIM
MOBA_BLOCK = 256
MOBA_TOPK = 3
PEER_HEADS = 8
PEER_NKEYS = 128
PEER_TOPK = 16
RMS_EPS = 1e-6
NEG = -1e30
LANES = 128
SUBLANES = 8
BF16_ROWS = 2 * SUBLANES
VMEM_LIMIT = 56 * 1024 * 1024

_NT = (((1,), (1,)), ((), ()))


def _dot(a, b):
    return jnp.dot(a, b, preferred_element_type=F32)


def _dot_nt(a, b, precision=None):
    return lax.dot_general(a, b, _NT, precision=precision, preferred_element_type=F32)


def _params(*sem):
    return pltpu.CompilerParams(dimension_semantics=sem, vmem_limit_bytes=VMEM_LIMIT)


def _half(idx):
    return jnp.where(idx >= HEAD_DIM, 1, 0)


def _rms(x, g):
    return x * lax.rsqrt(jnp.mean(x * x, axis=-1, keepdims=True) + RMS_EPS) * g


def _proj_kernel(x_ref, g_ref, wqkv_ref, wfh_ref, wfl_ref, bf_ref, wg_ref, wkvt_ref,
                 qa_ref, ka_ref, va_ref, qb_ref, kb_ref, vb_ref, lf_ref, ga_ref, gb_ref, *kvt_refs):
    xn = _rms(x_ref[...], g_ref[...])
    xb = xn.astype(BF16)
    width = qa_ref.shape[-1]
    for i, o_ref in enumerate((qa_ref, ka_ref, va_ref, qb_ref, kb_ref, vb_ref)):
        o_ref[...] = _dot(xb, wqkv_ref[:, i * width:(i + 1) * width])
    for i, t_ref in enumerate(kvt_refs):
        t_ref[...] = _dot_nt(wkvt_ref[i * width:(i + 1) * width, :], xb)
    xl = (xn - xb.astype(F32)).astype(BF16)
    zf = _dot(xb, wfh_ref[...]) + (_dot(xl, wfh_ref[...]) + _dot(xb, wfl_ref[...]))
    zf = zf[:, :N_HEADS] + bf_ref[...]
    lf_ref[...] = jnp.minimum(zf, 0.0) - jnp.log1p(jnp.exp(-jnp.abs(zf)))
    d = ga_ref.shape[-1]
    ga_ref[...] = 1.0 / (1.0 + jnp.exp(-_dot(xb, wg_ref[:, :d])))
    gb_ref[...] = 1.0 / (1.0 + jnp.exp(-_dot(xb, wg_ref[:, d:])))


def _proj(x, g, wqkv, wfh, wfl, bfg, wg, wkvt, tm, batch=None):
    n, d = x.shape
    width = wqkv.shape[1] // 6
    row = lambda i: (i, 0)
    fix = lambda i: (0, 0)
    outs = [jax.ShapeDtypeStruct((n, width), F32)] * 6 + [
        jax.ShapeDtypeStruct((n, N_HEADS), F32),
        jax.ShapeDtypeStruct((n, d), F32), jax.ShapeDtypeStruct((n, d), F32)]
    out_specs = [pl.BlockSpec((tm, width), row)] * 6 + [
        pl.BlockSpec((tm, N_HEADS), row), pl.BlockSpec((tm, d), row), pl.BlockSpec((tm, d), row)]
    if batch is not None:
        seq = n // batch
        nq = seq // tm
        outs += [jax.ShapeDtypeStruct((batch, width, seq), F32)] * 4
        out_specs += [pl.BlockSpec((None, width, tm), lambda i: (i // nq, 0, i % nq))] * 4
    return pl.pallas_call(
        _proj_kernel,
        grid=(n // tm,),
        in_specs=[pl.BlockSpec((tm, d), row), pl.BlockSpec((1, d), fix),
                  pl.BlockSpec(wqkv.shape, fix), pl.BlockSpec(wfh.shape, fix),
                  pl.BlockSpec(wfl.shape, fix), pl.BlockSpec((1, N_HEADS), fix),
                  pl.BlockSpec(wg.shape, fix), pl.BlockSpec(wkvt.shape, fix)],
        out_specs=out_specs,
        out_shape=outs,
        compiler_params=_params("parallel"),
        name="proj",
    )(x, g, wqkv, wfh, wfl, bfg, wg, wkvt)


def _cumsum_kernel(x_ref, o_ref, carry_ref):
    @pl.when(pl.program_id(0) == 0)
    def _():
        carry_ref[...] = jnp.zeros_like(carry_ref)

    blk = x_ref.shape[0]
    r = lax.broadcasted_iota(jnp.int32, (blk, blk), 0)
    c = lax.broadcasted_iota(jnp.int32, (blk, blk), 1)
    tri = jnp.where(c <= r, 1.0, 0.0).astype(F32)
    y = jnp.dot(tri, x_ref[...], precision=lax.Precision.HIGHEST,
                preferred_element_type=F32) + carry_ref[...]
    o_ref[...] = y
    carry_ref[...] = y[blk - 1:blk, :]


def _cumsum_rows(x, blk=256):
    n, c = x.shape
    return pl.pallas_call(
        _cumsum_kernel,
        grid=(n // blk,),
        in_specs=[pl.BlockSpec((blk, c), lambda i: (i, 0))],
        out_specs=pl.BlockSpec((blk, c), lambda i: (i, 0)),
        out_shape=jax.ShapeDtypeStruct((n, c), F32),
        scratch_shapes=[pltpu.VMEM((1, c), F32)],
        compiler_params=_params("arbitrary"),
        name="cumsum",
    )(x)


def _stage_kv(k_ref, v_ref, kb_s, vt_s, kmean_s):
    nblk = k_ref.shape[0] // MOBA_BLOCK
    for c in range(nblk):
        rows = slice(c * MOBA_BLOCK, (c + 1) * MOBA_BLOCK)
        kc = k_ref[rows, :]
        if kb_s is not None:
            kb_s[rows, :] = kc.astype(BF16)
        if kmean_s is not None:
            kmean_s[c:c + 1, :] = jnp.sum(kc, axis=0, keepdims=True) * (1.0 / MOBA_BLOCK)
        vt_s[:, rows] = v_ref[rows, :].T.astype(BF16)


BIG = 1e30
LOG2E = 1.4426950408889634
PAST, OWN, DEAD = 0, 1, 2


def _block_rows(blk):
    return pl.ds(pl.multiple_of(blk * MOBA_BLOCK, MOBA_BLOCK), MOBA_BLOCK)


def _flash_sweep(own, qs, kb_fn, vt_s, bias_s, row_fn, qk_s, p_s, acc_s, o_ref):
    nhq = qs[0].shape[0] // LANES
    chains = [(e, hq) for e in range(2) for hq in range(nhq)]
    rows = _block_rows
    block_of = lambda t: jnp.where(t == 0, own, jnp.maximum(jnp.minimum(t, own) - 1, 0))

    def issue_qk(blk, slot):
        for e in range(2):
            raw = _dot_nt(kb_fn(e, blk), qs[e])
            for hq in range(nhq):
                qk_s[slot, e * nhq + hq] = raw[:, _lane_half(hq)]

    def issue_pv(blk, slot, alphas):
        for e in range(2):
            pv = _dot(vt_s[_head_rows(e), rows(blk)], p_s[slot, e])
            for hq in range(nhq):
                c = e * nhq + hq
                acc_s[c] = alphas[c] * acc_s[c] + pv[:, _lane_half(hq)]

    acc_s[...] = jnp.zeros_like(acc_s)
    p_s[1] = jnp.zeros(p_s.shape[1:], p_s.dtype)
    issue_qk(own, 0)
    row = lambda v: jnp.full((1, LANES), v, F32)
    init = (tuple(row(NEG) for _ in chains), tuple(row(0.0) for _ in chains), tuple(row(1.0) for _ in chains))

    def trip(t, slot, carry):
        ms, ls, alphas = carry
        issue_pv(block_of(jnp.maximum(t - 1, 0)), 1 - slot, alphas)
        issue_qk(block_of(t + 1), 1 - slot)
        blk = block_of(t)
        kind = jnp.where(t > own, DEAD, jnp.where(t == 0, OWN, PAST))
        off = (own - blk) * MOBA_BLOCK
        new_m, new_l, new_a = [], [], []
        for c, (e, hq) in enumerate(chains):
            lg = qk_s[slot, c] - bias_s[kind, c]
            if row_fn is not None:
                lg = lg + row_fn(e, hq, blk, off)
            mn = jnp.maximum(ms[c], jnp.max(lg, axis=0, keepdims=True))
            alpha = jnp.exp2(ms[c] - mn)
            p = jnp.exp2(lg - mn)
            p_s[slot, e, :, _lane_half(hq)] = p.astype(BF16)
            new_m.append(mn)
            new_l.append(alpha * ls[c] + jnp.sum(p, axis=0, keepdims=True))
            new_a.append(alpha)
        return tuple(new_m), tuple(new_l), tuple(new_a)

    n_pairs = own // 2 + 1
    ms, ls, alphas = lax.fori_loop(0, n_pairs, lambda g, c: trip(2 * g + 1, 1, trip(2 * g, 0, c)), init)
    issue_pv(block_of(2 * n_pairs - 1), 1, alphas)
    heads = [jnp.concatenate([acc_s[e * nhq + hq] / ls[e * nhq + hq] for hq in range(nhq)], axis=1)
             for e in range(2)]
    o_ref[...] = jnp.concatenate(heads, axis=0).T


def _head_rows(e):
    return slice(e * HEAD_DIM, (e + 1) * HEAD_DIM)


def _lane_half(hq):
    return slice(hq * LANES, (hq + 1) * LANES)


def _sweep_scratch(tq):
    nchain = 2 * (tq // LANES)
    return [pltpu.VMEM((3, nchain, MOBA_BLOCK, LANES), F32),
            pltpu.VMEM((2, nchain, MOBA_BLOCK, LANES), F32), pltpu.VMEM((2, 2, MOBA_BLOCK, tq), BF16),
            pltpu.VMEM((nchain, HEAD_DIM, LANES), F32)]


def _rel_pos(hq):
    return (lax.broadcasted_iota(jnp.int32, (MOBA_BLOCK, LANES), 1) + hq * LANES
            - lax.broadcasted_iota(jnp.int32, (MOBA_BLOCK, LANES), 0))


def _moba_kernel(slopes_ref, q_ref, k_ref, v_ref, o_ref, kb_s, vt_s, kmean_s, sel_s, bias_s, qk_s, p_s, acc_s):
    hp = pl.program_id(1)
    own = pl.program_id(2)
    tq = q_ref.shape[0]
    nblk = kmean_s.shape[0]
    nhq = tq // LANES
    slopes2 = [slopes_ref[hp * 2 + e] * LOG2E for e in range(2)]

    @pl.when(own == 0)
    def _():
        _stage_kv(k_ref, v_ref, kb_s, vt_s, kmean_s)
        for e in range(2):
            for hq in range(nhq):
                rel = _rel_pos(hq)
                alibi = slopes2[e] * rel.astype(F32)
                bias_s[PAST, e * nhq + hq] = alibi
                bias_s[OWN, e * nhq + hq] = jnp.where(rel >= 0, alibi, BIG)
                bias_s[DEAD, e * nhq + hq] = jnp.full(alibi.shape, BIG, F32)

    q = q_ref[...]
    lane_head = _half(lax.broadcasted_iota(jnp.int32, (1, PAIR), 1))
    blk = lax.broadcasted_iota(jnp.int32, (nblk, tq), 0)
    qs = []
    for e in range(2):
        qm = jnp.where(lane_head == e, q, 0.0)
        gate = _dot_nt(kmean_s[...], qm, precision=lax.Precision.HIGHEST)
        valid = blk < own
        gate = jnp.where(valid, gate, -jnp.inf)
        rank = jnp.zeros((nblk, tq), F32)
        for mth in range(nblk):
            g_m = gate[mth:mth + 1, :]
            beats = jnp.where(g_m > gate, 1.0, jnp.where((g_m == gate) & (blk > mth), 1.0, 0.0))
            rank = rank + beats
        sel = jnp.where((valid & (rank < MOBA_TOPK)) | (blk == own), 1.0, 0.0)
        for mth in range(nblk):
            sel_s[e, mth] = jnp.broadcast_to(sel[mth:mth + 1, :], sel_s.shape[2:])
        qe = (qm * (HEAD_DIM ** -0.5 * LOG2E)).astype(BF16)
        qs.append(qe)

    def row_fn(e, hq, b, off):
        picked = sel_s[e, b][0:1, _lane_half(hq)]
        return (picked - 1.0) * BIG - slopes2[e] * off.astype(F32)

    _flash_sweep(own, qs, lambda e, b: kb_s[_block_rows(b), :], vt_s, bias_s, row_fn, qk_s, p_s, acc_s, o_ref)


def _moba_prompt(slopes, q, k, v, batch, seq):
    n, width = q.shape
    tq = MOBA_BLOCK
    nq = seq // tq
    npair = width // PAIR
    return pl.pallas_call(
        _moba_kernel,
        grid=(batch, npair, nq),
        in_specs=[pl.BlockSpec(memory_space=pltpu.SMEM),
                  pl.BlockSpec((tq, PAIR), lambda b, h, i: (b * nq + i, h)),
                  pl.BlockSpec((seq, PAIR), lambda b, h, i: (b, h)),
                  pl.BlockSpec((seq, PAIR), lambda b, h, i: (b, h))],
        out_specs=pl.BlockSpec((tq, PAIR), lambda b, h, i: (b * nq + i, h)),
        out_shape=jax.ShapeDtypeStruct((n, width), F32),
        scratch_shapes=[pltpu.VMEM((seq, PAIR), BF16), pltpu.VMEM((PAIR, seq), BF16),
                        pltpu.VMEM((seq // MOBA_BLOCK, PAIR), F32),
                        pltpu.VMEM((2, seq // MOBA_BLOCK, SUBLANES, tq), F32)] + _sweep_scratch(tq),
        compiler_params=_params("parallel", "parallel", "arbitrary"),
        name="moba",
    )(slopes, q, k, v)


def _split3(x):
    hi = x.astype(BF16).astype(F32)
    mid = (x - hi).astype(BF16).astype(F32)
    lo = ((x - hi) - mid).astype(BF16).astype(F32)
    return hi, mid, lo


def _with_bias_lanes(x, e, ones_at, column, column_at):
    lane = lax.broadcasted_iota(jnp.int32, (1, PAIR), 1)
    base = (1 - e) * HEAD_DIM
    out = jnp.where(_half(lane) == e, x, 0.0)
    out = jnp.where((lane >= base + ones_at) & (lane < base + ones_at + 3), 1.0, out)
    for i, part in enumerate(_split3(column)):
        out = jnp.where(lane == base + column_at + i, part, out)
    return out


def _fox_kernel(q_ref, k_ref, v_ref, c_ref, o_ref, kb_s, vt_s, bias_s, qk_s, p_s, acc_s):
    own = pl.program_id(2)
    tq = q_ref.shape[0]
    nhq = tq // LANES

    @pl.when(own == 0)
    def _():
        _stage_kv(k_ref, v_ref, None, vt_s, None)
        for blk in range(k_ref.shape[0] // MOBA_BLOCK):
            rows = slice(blk * MOBA_BLOCK, (blk + 1) * MOBA_BLOCK)
            for e in range(2):
                kb_s[e, rows, :] = _with_bias_lanes(k_ref[rows, :], e, 3, c_ref[rows, e:e + 1] * -LOG2E, 0).astype(BF16)
        for c in range(2 * nhq):
            bias_s[PAST, c] = jnp.zeros(bias_s.shape[2:], F32)
            bias_s[OWN, c] = jnp.where(_rel_pos(c % nhq) >= 0, 0.0, BIG)
            bias_s[DEAD, c] = jnp.full(bias_s.shape[2:], BIG, F32)

    q = q_ref[...] * (HEAD_DIM ** -0.5 * LOG2E)
    qs = [_with_bias_lanes(q, e, 0, c_ref[_block_rows(own), e:e + 1] * LOG2E, 3).astype(BF16) for e in range(2)]
    _flash_sweep(own, qs, lambda e, b: kb_s[e, _block_rows(b), :], vt_s, bias_s, None, qk_s, p_s, acc_s, o_ref)


def _fox_prompt(q, k, v, c, batch, seq):
    n, width = q.shape
    tq = MOBA_BLOCK
    nq = seq // tq
    npair = width // PAIR
    return pl.pallas_call(
        _fox_kernel,
        grid=(batch, npair, nq),
        in_specs=[pl.BlockSpec((tq, PAIR), lambda b, h, i: (b * nq + i, h)),
                  pl.BlockSpec((seq, PAIR), lambda b, h, i: (b, h)),
                  pl.BlockSpec((seq, PAIR), lambda b, h, i: (b, h)),
                  pl.BlockSpec((None, None, seq, 2), lambda b, h, i: (b, h, 0, 0))],
        out_specs=pl.BlockSpec((tq, PAIR), lambda b, h, i: (b * nq + i, h)),
        out_shape=jax.ShapeDtypeStruct((n, width), F32),
        scratch_shapes=[pltpu.VMEM((2, seq, PAIR), BF16), pltpu.VMEM((PAIR, seq), BF16)] + _sweep_scratch(tq),
        compiler_params=_params("parallel", "parallel", "arbitrary"),
        name="fox",
    )(q, k, v, c)


def _merge_kernel(x_ref, oa_ref, ob_ref, ga_ref, gb_ref, wa_ref, wb_ref, wo_ref, g_ref, wq_ref, sk_ref,
                  h_ref, hn_ref, st_ref):
    a = _dot(oa_ref[...].astype(BF16), wa_ref[...])
    b = _dot(ob_ref[...].astype(BF16), wb_ref[...])
    mix = ga_ref[...] * a + gb_ref[...] * b
    h = x_ref[...] + _dot(mix.astype(BF16), wo_ref[...])
    h_ref[...] = h
    hn = _rms(h, g_ref[...]).astype(BF16)
    hn_ref[...] = hn
    for i in range(0, st_ref.shape[0], 2):
        qp = _dot(hn, wq_ref[:, i * PEER_NKEYS:(i + 2) * PEER_NKEYS]).astype(BF16)
        for j in range(2):
            st_ref[i + j] = _dot_nt(sk_ref[i + j], qp[:, j * PEER_NKEYS:(j + 1) * PEER_NKEYS])


def _merge(x, oa, ob, ga, gb, wa, wb, wo, g, wq, sk, tm):
    n, d = x.shape
    width = oa.shape[1]
    nsk = sk.shape[0]
    row = lambda i: (i, 0)
    fix = lambda i: (0, 0)
    return pl.pallas_call(
        _merge_kernel,
        grid=(n // tm,),
        in_specs=[pl.BlockSpec((tm, d), row), pl.BlockSpec((tm, width), row), pl.BlockSpec((tm, width), row),
                  pl.BlockSpec((tm, d), row), pl.BlockSpec((tm, d), row),
                  pl.BlockSpec(wa.shape, fix), pl.BlockSpec(wb.shape, fix), pl.BlockSpec(wo.shape, fix),
                  pl.BlockSpec((1, d), fix), pl.BlockSpec(wq.shape, fix),
                  pl.BlockSpec(sk.shape, lambda i: (0, 0, 0))],
        out_specs=[pl.BlockSpec((tm, d), row), pl.BlockSpec((tm, d), row),
                   pl.BlockSpec((nsk, PEER_NKEYS, tm), lambda i: (0, 0, i))],
        out_shape=[jax.ShapeDtypeStruct((n, d), F32), jax.ShapeDtypeStruct((n, d), BF16),
                   jax.ShapeDtypeStruct((nsk, PEER_NKEYS, n), F32)],
        compiler_params=_params("parallel"),
        name="merge",
    )(x, oa, ob, ga, gb, wa, wb, wo, g, wq, sk)


def _cand_rows():
    rows = []
    for j in range(PEER_TOPK):
        rows.append(PEER_TOPK // (j + 1))
    return rows


def _batcher_network(n):
    def merge(lo, hi, r):
        step = r * 2
        if step < hi - lo:
            yield from merge(lo, hi, step)
            yield from merge(lo + r, hi, step)
            yield from [(i, i + r) for i in range(lo + r, hi - r, step)]
        else:
            yield (lo, lo + r)

    def sort(lo, hi):
        if hi - lo >= 1:
            mid = lo + (hi - lo) // 2
            yield from sort(lo, mid)
            yield from sort(mid + 1, hi)
            yield from merge(lo, hi, 1)

    return list(sort(0, n - 1))


def _top_sorted(vals, count, dst_ref=None):
    ngrp = vals.shape[0] // SUBLANES
    v = [vals[k * SUBLANES:(k + 1) * SUBLANES, :] for k in range(ngrp)]
    while len(v) & (len(v) - 1):
        v.append(jnp.full(v[0].shape, -jnp.inf, F32))
    for i, j in _batcher_network(len(v)):
        v[i], v[j] = jnp.maximum(v[i], v[j]), jnp.minimum(v[i], v[j])
    v.append(jnp.full(v[0].shape, -jnp.inf, F32))
    sub = lax.broadcasted_iota(jnp.int32, v[0].shape, 0).astype(F32)
    out = []
    for r in range(count):
        mx = jnp.max(v[0], axis=0, keepdims=True)
        out.append(mx)
        if dst_ref is not None:
            dst_ref[r:r + 1, :] = mx
        first = jnp.min(jnp.where(v[0] == mx, sub, float(SUBLANES)), axis=0, keepdims=True)
        popped = sub == first
        for k in range(min(count - 1 - r, len(v) - 1)):
            v[k] = jnp.where(popped, v[k + 1], v[k])
    return out


def _bf16_twice(x):
    bits = pltpu.bitcast(x.astype(BF16).astype(F32), jnp.uint32)
    return bits | lax.shift_right_logical(bits, jnp.uint32(16))


def _select_head(h, st_ref, rank2_ref, f2_ref, cnt_ref, c1_ref, a_s, b_s, cand_s):
    rows = _cand_rows()
    s1 = st_ref[2 * h]
    s2 = st_ref[2 * h + 1]
    _top_sorted(s1, PEER_TOPK, a_s)
    _top_sorted(s2, PEER_TOPK, b_s)
    a = a_s[...]
    b = b_s[...]
    cand_s[...] = jnp.full(cand_s.shape, -jnp.inf, F32)
    off = 0
    for j, lim in enumerate(rows):
        cand_s[off:off + lim, :] = a[0:lim, :] + b[j:j + 1, :]
        off += lim
    tops = _top_sorted(cand_s[...], PEER_TOPK)
    theta = tops[PEER_TOPK - 1]
    z = jnp.ones_like(theta)
    for t in tops[1:]:
        z = z + jnp.exp(t - tops[0])
    cnt = jnp.zeros(s1.shape, F32)
    rank2 = jnp.zeros(s2.shape, F32)
    for j in range(PEER_TOPK):
        bj = b[j:j + 1, :]
        cnt = cnt + jnp.where(s1 + bj >= theta, 1.0, 0.0)
        rank2 = rank2 + jnp.where(bj > s2, 1.0, 0.0)
    cnt_ref[h] = _bf16_twice(cnt)
    rank2_ref[h] = pltpu.bitcast(rank2.astype(BF16), jnp.uint32)
    f2_ref[h] = pltpu.bitcast(jnp.exp(s2 - b[0:1, :]).astype(BF16), jnp.uint32)
    c1_ref[h] = _bf16_twice(jnp.exp(s1 - a[0:1, :]) / z)


def _select_kernel(st_ref, rank2_ref, f2_ref, cnt_ref, c1_ref, *scratch):
    def pair(hp, carry):
        for e in range(2):
            _select_head(2 * hp + e, st_ref, rank2_ref, f2_ref, cnt_ref, c1_ref, *scratch[3 * e:3 * e + 3])
        return carry

    lax.fori_loop(0, PEER_HEADS // 2, pair, 0)


def _select(st, tl=LANES):
    nsk, nk, n = st.shape
    spec_in = pl.BlockSpec((nsk, nk, tl), lambda i: (0, 0, i))
    spec_out = lambda rows: pl.BlockSpec((PEER_HEADS, rows, tl), lambda i: (0, 0, i))
    ncand = -(-sum(_cand_rows()) // SUBLANES) * SUBLANES
    scratch = [pltpu.VMEM((PEER_TOPK, tl), F32), pltpu.VMEM((PEER_TOPK, tl), F32), pltpu.VMEM((ncand, tl), F32)]
    shape = lambda rows: jax.ShapeDtypeStruct((PEER_HEADS, rows, n), jnp.uint32)
    return pl.pallas_call(
        _select_kernel,
        grid=(n // tl,),
        in_specs=[spec_in],
        out_specs=[spec_out(nk // 2), spec_out(nk // 2), spec_out(nk), spec_out(nk)],
        out_shape=[shape(nk // 2), shape(nk // 2), shape(nk), shape(nk)],
        scratch_shapes=scratch * 2,
        compiler_params=_params("parallel"),
        name="select",
    )(st)


def _gelu(x):
    return 0.5 * x * (1.0 + jnp.tanh(0.7978845608028654 * (x + 0.044715 * (x * x * x))))


def _peer_weights(il, tc, g0, rank2_ref, f2_ref, cnt_ref, c1_ref):
    cols = slice(tc * LANES, (tc + 1) * LANES)
    cnt, c1 = [], []
    for h in range(PEER_HEADS):
        grp = pl.ds(pl.multiple_of(h * PEER_NKEYS + g0, SUBLANES), SUBLANES)
        for ref, dst in ((cnt_ref, cnt), (c1_ref, c1)):
            row = jnp.broadcast_to(ref[grp, cols][il:il + 1, :], (SUBLANES, LANES))
            dst.append(pltpu.bitcast(row, BF16))
    slabs = []
    for r in range(PEER_NKEYS // BF16_ROWS):
        w = jnp.zeros((BF16_ROWS, LANES), BF16)
        for h in range(PEER_HEADS):
            words = slice((h * PEER_NKEYS + r * BF16_ROWS) // 2, (h * PEER_NKEYS + (r + 1) * BF16_ROWS) // 2)
            rank2 = pltpu.bitcast(rank2_ref[words, cols], BF16)
            f2 = pltpu.bitcast(f2_ref[words, cols], BF16)
            w = w + jnp.where(rank2 < cnt[h], f2 * c1[h], 0.0)
        slabs.append(w)
    return slabs


def _peer_kernel(hn_ref, h_ref, rank2_ref, f2_ref, cnt_ref, c1_ref, u_ref, vtp_ref, vtc_ref, g_ref, y_ref,
                 acc_s, s_s, pa_s, pb_s):
    ec = pl.program_id(1)
    last = pl.num_programs(1) - 1
    ecn, tb = pa_s.shape
    n_i1 = ecn // PEER_NKEYS
    n_tc = tb // LANES
    assert n_i1 == SUBLANES
    n_half = 2 if n_tc % 2 == 0 else 1
    halves = [slice(k * (tb // n_half), (k + 1) * (tb // n_half)) for k in range(n_half)]

    @pl.when(ec == 0)
    def _():
        acc_s[...] = jnp.zeros_like(acc_s)
        pb_s[...] = jnp.zeros_like(pb_s)

    def step(p_prev, p_cur):
        g0 = pl.multiple_of(ec * n_i1, SUBLANES)
        for cols in halves:
            s_s[:, cols] = _dot_nt(u_ref[...], hn_ref[cols, :])
        acc_s[...] += _dot(vtp_ref[...], p_prev[...])
        for tc in range(n_tc):
            cols = slice(tc * LANES, (tc + 1) * LANES)
            for il in range(n_i1):
                for r, w in enumerate(_peer_weights(il, tc, g0, rank2_ref, f2_ref, cnt_ref, c1_ref)):
                    rows = slice(il * PEER_NKEYS + r * BF16_ROWS, il * PEER_NKEYS + (r + 1) * BF16_ROWS)
                    p_cur[rows, cols] = w * _gelu(s_s[rows, cols].astype(BF16))

        @pl.when(ec == last)
        def _():
            y_ref[...] = _rms(h_ref[...] + (acc_s[...] + _dot(vtc_ref[...], p_cur[...])).T, g_ref[...])

    pl.when(ec % 2 == 0)(lambda: step(pb_s, pa_s))
    pl.when(ec % 2 == 1)(lambda: step(pa_s, pb_s))


def _peer(hn, h, rank2, f2, cnt, c1, u, vt, g, tb, ecn):
    n, d = h.shape
    nchunk = u.shape[0] // ecn
    assert nchunk >= 2
    tok = lambda t, e: (t, 0)
    tables = [z.reshape(-1, n) for z in (rank2, f2, cnt, c1)]
    sel = [pl.BlockSpec((z.shape[0], tb), lambda t, e: (0, t)) for z in tables]
    rank2, f2, cnt, c1 = tables
    return pl.pallas_call(
        _peer_kernel,
        grid=(n // tb, nchunk),
        in_specs=[pl.BlockSpec((tb, d), tok), pl.BlockSpec((tb, d), tok)] + sel + [
                  pl.BlockSpec((ecn, d), lambda t, e: (e, 0)),
                  pl.BlockSpec((d, ecn), lambda t, e: (0, jnp.maximum(e - 1, 0))),
                  pl.BlockSpec((d, ecn), lambda t, e: (0, (e // (nchunk - 1)) * (nchunk - 1))),
                  pl.BlockSpec((1, d), lambda t, e: (0, 0))],
        out_specs=pl.BlockSpec((tb, d), tok),
        out_shape=jax.ShapeDtypeStruct((n, d), F32),
        scratch_shapes=[pltpu.VMEM((d, tb), F32), pltpu.VMEM((ecn, tb), F32),
                        pltpu.VMEM((ecn, tb), BF16), pltpu.VMEM((ecn, tb), BF16)],
        compiler_params=_params("parallel", "arbitrary"),
        name="peer",
    )(hn, h, rank2, f2, cnt, c1, u, vt, vt, g)


def _page_index(per_step, i):
    def index(s, p, pt):
        return (0, pt[s, p * per_step + i], 0, 0, 0)
    return index


def _gather_pages_kernel(pt_ref, *refs):
    o_ref = refs[-1]
    for i in range(len(refs) - 1):
        o_ref[i] = refs[i][...]


def _gather_pages(page_table, cache_t, per_step=32):
    nseq, npages = page_table.shape
    heads, page = cache_t.shape[2:]
    per_step = min(per_step, npages)
    grid_spec = pltpu.PrefetchScalarGridSpec(
        num_scalar_prefetch=1,
        grid=(nseq, npages // per_step),
        in_specs=[pl.BlockSpec((None, None, heads, page),
                               (lambda i: lambda s, p, pt: (0, pt[s, p * per_step + i], 0, 0))(i))
                  for i in range(per_step)],
        out_specs=pl.BlockSpec((None, per_step, heads, page), lambda s, p, pt: (s, p, 0, 0)),
    )
    return pl.pallas_call(
        _gather_pages_kernel, grid_spec=grid_spec,
        out_shape=jax.ShapeDtypeStruct((nseq, npages, heads, page), F32),
        compiler_params=_params("parallel", "arbitrary"),
        name="gather_logf",
    )(page_table, *([cache_t] * per_step))


def _fox_sample_kernel(pt_ref, q_ref, kn_ref, vn_ref, cq_ref, ck_ref, *refs):
    n_pg = (len(refs) - 4) // 2
    k_refs = refs[:n_pg]
    v_refs = refs[n_pg:2 * n_pg]
    o_ref, m_s, l_s, acc_s = refs[2 * n_pg:]
    step = pl.program_id(1)
    heads = q_ref.shape[0]

    @pl.when(step == 0)
    def _():
        m_s[...] = jnp.full(m_s.shape, NEG, F32)
        l_s[...] = jnp.zeros_like(l_s)
        acc_s[...] = jnp.zeros_like(acc_s)

    for h in range(heads):
        q = q_ref[h] * HEAD_DIM ** -0.5
        bias_q = cq_ref[h:h + 1, :]
        lgs = [jnp.sum(q * k_refs[i][h], axis=0, keepdims=True) + (bias_q - ck_ref[i, h:h + 1, :])
               for i in range(n_pg)]
        top = lgs[0]
        for lg in lgs[1:]:
            top = jnp.maximum(top, lg)
        m = m_s[h:h + 1, :]
        mn = jnp.maximum(m, jnp.max(top, axis=1, keepdims=True))
        alpha = jnp.exp(m - mn)
        l = alpha * l_s[h:h + 1, :]
        acc = alpha * acc_s[h]
        for i in range(n_pg):
            p = jnp.exp(lgs[i] - mn)
            l = l + p
            acc = acc + p * v_refs[i][h]
        m_s[h:h + 1, :] = mn
        l_s[h:h + 1, :] = l
        acc_s[h] = acc

    @pl.when(step == pl.num_programs(1) - 1)
    def _():
        for h in range(heads):
            q = q_ref[h] * HEAD_DIM ** -0.5
            lg_new = jnp.sum(q * kn_ref[h], axis=0, keepdims=True)
            m = m_s[h:h + 1, :]
            mf = jnp.maximum(m, lg_new)
            a = jnp.exp(m - mf)[:, 0:1]
            b = jnp.exp(lg_new - mf)[:, 0:1]
            num = a * jnp.sum(acc_s[h], axis=1, keepdims=True) + b * vn_ref[h][:, 0:1]
            den = a * jnp.sum(l_s[h:h + 1, :], axis=1, keepdims=True) + b
            o_ref[h] = num / den


def _fox_sample(page_table, qb, knb, vnb, cq, ck, cache_k, cache_v, per_step=16):
    nseq, npages = page_table.shape
    heads, hd, page = cache_k.shape[2:]
    per_step = min(per_step, npages)
    tok = pl.BlockSpec((None, heads, hd, LANES), lambda s, p, pt: (s, 0, 0, 0))
    pages = [pl.BlockSpec((None, None, heads, hd, page), _page_index(per_step, i)) for i in range(per_step)]
    grid_spec = pltpu.PrefetchScalarGridSpec(
        num_scalar_prefetch=1,
        grid=(nseq, npages // per_step),
        in_specs=[tok, tok, tok,
                  pl.BlockSpec((None, heads, LANES), lambda s, p, pt: (s, 0, 0)),
                  pl.BlockSpec((None, per_step, heads, page), lambda s, p, pt: (s, p, 0, 0))] + pages + pages,
        out_specs=pl.BlockSpec((None, heads, hd, 1), lambda s, p, pt: (s, 0, 0, 0)),
        scratch_shapes=[pltpu.VMEM((heads, LANES), F32), pltpu.VMEM((heads, LANES), F32),
                        pltpu.VMEM((heads, hd, LANES), F32)],
    )
    return pl.pallas_call(
        _fox_sample_kernel, grid_spec=grid_spec,
        out_shape=jax.ShapeDtypeStruct((nseq, heads, hd, 1), F32),
        compiler_params=_params("parallel", "arbitrary"),
        name="fox_sample",
    )(page_table, qb, knb, vnb, cq, ck, *([cache_k] * per_step), *([cache_v] * per_step))


def _moba_gate_kernel(pt_ref, q_ref, *refs):
    n_pg = len(refs) - 2
    k_refs = refs[:n_pg]
    sel_ref, gate_s = refs[n_pg:]
    step = pl.program_id(1)
    heads, _, page = k_refs[0].shape
    pages_per_block = MOBA_BLOCK // page
    blocks_per_step = n_pg // pages_per_block
    for h in range(heads):
        q = q_ref[h]
        for i in range(blocks_per_step):
            t = jnp.sum(q * k_refs[i * pages_per_block][h], axis=0, keepdims=True)
            for j in range(1, pages_per_block):
                t = t + jnp.sum(q * k_refs[i * pages_per_block + j][h], axis=0, keepdims=True)
            gate = jnp.sum(t, axis=1, keepdims=True) * (1.0 / MOBA_BLOCK)
            gate_s[step * blocks_per_step + i, h:h + 1, :] = jnp.broadcast_to(gate, (1, LANES))

    @pl.when(step == pl.num_programs(1) - 1)
    def _():
        g = gate_s[...]
        blk = lax.broadcasted_iota(jnp.int32, g.shape, 0)
        for r in range(MOBA_TOPK):
            mx = jnp.max(g, axis=0, keepdims=True)
            first = jnp.min(jnp.where(g == mx, blk, g.shape[0]), axis=0, keepdims=True)
            sel_ref[r] = first[0]
            g = jnp.where(blk == first, -jnp.inf, g)


def _moba_gate(page_table, qb, cache_k, per_step=16):
    nseq, npages = page_table.shape
    heads, hd, page = cache_k.shape[2:]
    per_step = min(per_step, npages)
    nblk = npages * page // MOBA_BLOCK
    grid_spec = pltpu.PrefetchScalarGridSpec(
        num_scalar_prefetch=1,
        grid=(nseq, npages // per_step),
        in_specs=[pl.BlockSpec((None, heads, hd, LANES), lambda s, p, pt: (s, 0, 0, 0))]
        + [pl.BlockSpec((None, None, heads, hd, page), _page_index(per_step, i)) for i in range(per_step)],
        out_specs=pl.BlockSpec((None, MOBA_TOPK, heads, LANES), lambda s, p, pt: (s, 0, 0, 0)),
        scratch_shapes=[pltpu.VMEM((nblk, heads, LANES), F32)],
    )
    return pl.pallas_call(
        _moba_gate_kernel, grid_spec=grid_spec,
        out_shape=jax.ShapeDtypeStruct((nseq, MOBA_TOPK, heads, LANES), jnp.int32),
        compiler_params=_params("parallel", "arbitrary"),
        name="moba_gate",
    )(page_table, qb, *([cache_k] * per_step))


def _moba_sample_kernel(phys_ref, start_ref, slopes_ref, q_ref, kn_ref, vn_ref, *refs, past):
    n_pg = (len(refs) - 1) // 2
    k_refs = refs[:n_pg]
    v_refs = refs[n_pg:2 * n_pg]
    o_ref = refs[-1]
    s_id, h = pl.program_id(0), pl.program_id(1)
    heads = pl.num_programs(1)
    q = q_ref[...] * HEAD_DIM ** -0.5
    slope = slopes_ref[h]
    lane = lax.broadcasted_iota(jnp.int32, (1, k_refs[0].shape[-1]), 1)
    lg_new = jnp.sum(q * kn_ref[...], axis=0, keepdims=True)
    lgs = []
    top = None
    for i in range(n_pg):
        start = start_ref[(s_id * heads + h) * n_pg + i]
        dist = (past - start - lane).astype(F32)
        lg = jnp.sum(q * k_refs[i][...], axis=0, keepdims=True) - slope * dist
        lgs.append(lg)
        top = lg if top is None else jnp.maximum(top, lg)
    m = jnp.maximum(lg_new, jnp.max(top, axis=1, keepdims=True))
    b = jnp.exp(lg_new - m)
    l = jnp.zeros_like(top)
    acc = jnp.zeros(q.shape, F32)
    for i in range(n_pg):
        p = jnp.exp(lgs[i] - m)
        l = l + p
        acc = acc + p * v_refs[i][...]
    num = jnp.sum(acc, axis=1, keepdims=True) + b[:, 0:1] * vn_ref[...][:, 0:1]
    den = jnp.sum(l, axis=1, keepdims=True) + b[:, 0:1]
    o_ref[...] = num / den


def _moba_sample(phys, start, slopes, qb, knb, vnb, cache_k, cache_v, past):
    nseq, heads, hd, _ = qb.shape
    page = cache_k.shape[-1]
    n_pg = phys.shape[0] // (nseq * heads)
    tok = pl.BlockSpec((None, None, hd, LANES), lambda s, h, ph, st: (s, h, 0, 0))

    def pg(i):
        return pl.BlockSpec((None, None, None, hd, page),
                            lambda s, h, ph, st: (0, ph[(s * heads + h) * n_pg + i], h, 0, 0))

    pages = [pg(i) for i in range(n_pg)]
    grid_spec = pltpu.PrefetchScalarGridSpec(
        num_scalar_prefetch=2,
        grid=(nseq, heads),
        in_specs=[pl.BlockSpec(memory_space=pltpu.SMEM), tok, tok, tok] + pages + pages,
        out_specs=pl.BlockSpec((None, None, hd, 1), lambda s, h, ph, st: (s, h, 0, 0)),
    )
    return pl.pallas_call(
        functools.partial(_moba_sample_kernel, past=past), grid_spec=grid_spec,
        out_shape=jax.ShapeDtypeStruct((nseq, heads, hd, 1), F32),
        compiler_params=_params("parallel", "arbitrary"),
        name="moba_sample",
    )(phys, start, slopes, qb, knb, vnb, *([cache_k] * n_pg), *([cache_v] * n_pg))


def _pad_rows(x, rows):
    return jnp.pad(x, ((0, rows - x.shape[0]),) + ((0, 0),) * (x.ndim - 1))


def _layer_weights(l, g_mix, w_in, b_forget, w_branch_a, w_branch_b, w_out, g_ffn, w_peer_q, peer_sub_keys,
                   peer_u, peer_v):
    d = w_in.shape[1]
    wa_w = w_branch_a.shape[1]
    n_qkv = 6 * wa_w
    w = w_in[l]
    wf = jnp.pad(w[:, n_qkv:n_qkv + N_HEADS], ((0, 0), (0, LANES - N_HEADS)))
    wfh = wf.astype(BF16)
    wfl = (wf - wfh.astype(F32)).astype(BF16)
    sk = peer_sub_keys[l]
    wkvt = jnp.concatenate([w[:, i * wa_w:(i + 1) * wa_w] for i in (1, 2, 4, 5)], axis=1).T.astype(BF16)
    return dict(
        g_mix=g_mix[l].reshape(1, d), wqkv=w[:, :n_qkv].astype(BF16), wfh=wfh, wfl=wfl, wkvt=wkvt,
        bf=b_forget[l].reshape(1, N_HEADS), wg=w[:, n_qkv + N_HEADS:].astype(BF16),
        wa=w_branch_a[l].astype(BF16), wb=w_branch_b[l].astype(BF16), wo=w_out[l].astype(BF16),
        g_ffn=g_ffn[l].reshape(1, d), wq=w_peer_q[l].astype(BF16),
        sk=sk.reshape((sk.shape[0] * sk.shape[1],) + sk.shape[2:]).astype(BF16),
        u=peer_u[l].astype(BF16), vt=peer_v[l].T.astype(BF16))


def _ffn(x, oa, ob, ga, gb, wts, g_out, tm, tb, ecn):
    h, hn, st = _merge(x, oa, ob, ga, gb, wts["wa"], wts["wb"], wts["wo"], wts["g_ffn"], wts["wq"], wts["sk"], tm)
    rank2, f2, cnt, c1 = _select(st)
    return _peer(hn, h, rank2, f2, cnt, c1, wts["u"], wts["vt"], g_out, tb, ecn)


def kernel(x_prompt, x_sample, cache_moba_k, cache_moba_v, cache_fox_k, cache_fox_v, cache_fox_logf, page_table,
           g_mix, w_in, b_forget, w_branch_a, w_branch_b, w_out, g_ffn, w_peer_q, peer_sub_keys, peer_u, peer_v,
           g_final):
    batch, seq, d = x_prompt.shape
    nseq, dec_seq, _ = x_sample.shape
    depth = w_in.shape[0]
    assert depth == 1 and dec_seq == 1, "one layer, one new token per sampled sequence"
    npages, page = page_table.shape[1], cache_moba_k.shape[2]
    past = npages * page
    assert past % MOBA_BLOCK == 0 and past // MOBA_BLOCK >= MOBA_TOPK and seq % MOBA_BLOCK == 0
    assert page == LANES
    slopes = 2.0 ** (-8.0 * jnp.arange(1, N_HEADS + 1, dtype=F32) / N_HEADS)
    wts = _layer_weights(0, g_mix, w_in, b_forget, w_branch_a, w_branch_b, w_out, g_ffn, w_peer_q, peer_sub_keys,
                         peer_u, peer_v)
    g_out = g_final.reshape(1, d)
    proj_w = (wts["g_mix"], wts["wqkv"], wts["wfh"], wts["wfl"], wts["bf"], wts["wg"], wts["wkvt"])
    npair = N_HEADS // 2
    ecn = SUBLANES * PEER_NKEYS

    n = batch * seq
    xp = x_prompt.reshape(n, d)
    qa, ka, va, qb, kb, vb, lf, ga, gb, *kv_t = _proj(xp, *proj_w, tm=min(256, seq), batch=batch)
    c = _cumsum_rows(lf.reshape(batch, seq, N_HEADS).transpose(1, 0, 2).reshape(seq, batch * N_HEADS))
    c = c.reshape(seq, batch, npair, 2)
    oa = _moba_prompt(slopes, qa, ka, va, batch, seq)
    ob = _fox_prompt(qb, kb, vb, c.transpose(1, 2, 0, 3), batch, seq)
    y_prompt = _ffn(xp, oa, ob, ga, gb, wts, g_out, min(256, n), min(512, n), ecn).reshape(batch, seq, d)
    p_new = tuple(z.reshape(1, batch, N_HEADS, HEAD_DIM, seq).transpose(0, 1, 4, 2, 3) for z in kv_t) + (
        lf.reshape(1, batch, seq, N_HEADS),)

    ns = -(-nseq // LANES) * LANES
    xs = _pad_rows(x_sample.reshape(nseq, d), ns)
    qa, ka, va, qb, kb, vb, lf, ga, gb = _proj(xs, *proj_w, tm=ns)
    lane_bcast = lambda z: jnp.broadcast_to(z[:nseq].reshape(nseq, N_HEADS, HEAD_DIM, 1),
                                            (nseq, N_HEADS, HEAD_DIM, LANES))
    pages_minor = lambda cache: cache.transpose(0, 1, 3, 4, 2)
    mk, mv = pages_minor(cache_moba_k), pages_minor(cache_moba_v)
    sel = _moba_gate(page_table, lane_bcast(qa), mk)[..., 0]
    ppb = MOBA_BLOCK // page
    sel_pages = sel.transpose(0, 2, 1)[..., None] * ppb + jnp.arange(ppb, dtype=jnp.int32)
    sel_pages = sel_pages.reshape(nseq, N_HEADS * MOBA_TOPK * ppb)
    phys = jnp.take_along_axis(page_table, sel_pages, axis=1).reshape(-1)
    start = (sel_pages * page).reshape(-1)
    oa = _moba_sample(phys, start, slopes, lane_bcast(qa), lane_bcast(ka), lane_bcast(va), mk, mv, past)
    lf_past = _gather_pages(page_table, cache_fox_logf.transpose(0, 1, 3, 2))
    lf_past = lf_past.transpose(0, 1, 3, 2).reshape(nseq, past, N_HEADS)
    lf_all = jnp.concatenate([lf_past, lf[:nseq, None, :]], axis=1)
    tot = -(-(past + 1) // MOBA_BLOCK) * MOBA_BLOCK
    lf_all = jnp.pad(lf_all, ((0, 0), (0, tot - past - 1), (0, 0)))
    cs = _cumsum_rows(lf_all.transpose(1, 0, 2).reshape(tot, nseq * N_HEADS)).reshape(tot, nseq, N_HEADS)
    cq = jnp.broadcast_to(cs[past][:, :, None], (nseq, N_HEADS, LANES))
    ck = cs[:past].reshape(npages, page, nseq, N_HEADS).transpose(2, 0, 3, 1)
    ob = _fox_sample(page_table, lane_bcast(qb), lane_bcast(kb), lane_bcast(vb), cq, ck,
                     pages_minor(cache_fox_k), pages_minor(cache_fox_v))
    oa = _pad_rows(oa.reshape(nseq, N_HEADS * HEAD_DIM), ns)
    ob = _pad_rows(ob.reshape(nseq, N_HEADS * HEAD_DIM), ns)
    y_sample = _ffn(xs, oa, ob, ga, gb, wts, g_out, ns, ns, ecn)[:nseq].reshape(nseq, 1, d)
    kv_shape = (1, nseq, 1, N_HEADS, HEAD_DIM)
    s_new = (ka[:nseq].reshape(kv_shape), va[:nseq].reshape(kv_shape), kb[:nseq].reshape(kv_shape),
             vb[:nseq].reshape(kv_shape), lf[:nseq].reshape(1, nseq, 1, N_HEADS))
    return (y_prompt, y_sample) + p_new + s_new
```
